```python
import math
import jax, jax.numpy as jnp
from jax import lax
import numpy as np

D_MODEL = 2048
BATCH = 4
SEQ = 4096
DEPTH = 2

CHUNK = 64
N_MIXERS = 2
EXPAND = 2
E_WIDTH = EXPAND * D_MODEL
EPS = 1e-6
M_HEAD_DIM = 64
M_HEADS = E_WIDTH // M_HEAD_DIM
M_GROUPS = 8
M_HEADS_PER_GROUP = M_HEADS // M_GROUPS
M_STATE = 128
M_CONV = 4
M_CONV_DIM = E_WIDTH + 2 * M_GROUPS * M_STATE
M_IN = 2 * E_WIDTH + 2 * M_GROUPS * M_STATE + M_HEADS
F_HEAD_DIM = 128
F_HEADS = E_WIDTH // F_HEAD_DIM
F_IN = 4 * E_WIDTH + F_HEADS
Q_BLOCK = 128
FORGET_BIAS_INIT = 5.0
N_MAMBA_LAYERS = (DEPTH + 1) // 2
N_FOX_LAYERS = DEPTH // 2

kernel_name = "hybrid_ssd_fox_streaming_encoder"


def rmsnorm(x, w):
    xf = x.astype(jnp.float32)
    y = xf * lax.rsqrt(jnp.mean(xf * xf, axis=-1, keepdims=True) + EPS)
    return (y * w.astype(jnp.float32)).astype(x.dtype)


def gated_group_rmsnorm(y, z, w):
    g = (y * jax.nn.silu(z)).astype(jnp.float32)
    shp = g.shape
    g = g.reshape(shp[:-1] + (M_GROUPS, shp[-1] // M_GROUPS))
    g = g * lax.rsqrt(jnp.mean(g * g, axis=-1, keepdims=True) + EPS)
    return (g.reshape(shp) * w.astype(jnp.float32)).astype(z.dtype)


def causal_depthwise_conv(u, w, bias):
    k = w.shape[0]
    out = lax.conv_general_dilated(
        u, w[:, None, :].astype(u.dtype), window_strides=(1,),
        padding=[(k - 1, 0)], dimension_numbers=('NWC', 'WIO', 'NWC'),
        feature_group_count=u.shape[-1])
    return out + bias.astype(u.dtype)


def ssd_chunked(xs, dt, a, bm, cm):
    xdt = xs * dt[..., None]
    acs = jnp.cumsum(dt * a, axis=2)
    causal = jnp.tril(jnp.ones((CHUNK, CHUNK), dtype=bool))
    seg = acs[:, :, :, None] - acs[:, :, None, :]
    lmat = jnp.exp(jnp.where(causal[:, :, None, None], seg, -jnp.inf))
    cb = jnp.einsum('bclgn,bcsgn->bclsg', cm, bm)
    y_diag = jnp.einsum('bclsgr,bcsgrp->bclgrp', cb[..., None] * lmat, xdt)
    decay_states = jnp.exp(acs[:, :, -1:] - acs)
    states = jnp.einsum('bclgn,bclgrp->bcgrpn', bm, xdt * decay_states[..., None])
    chunk_decay = jnp.exp(acs[:, :, -1])

    def step(carry, inp):
        st, dec = inp
        return carry * dec[..., None, None] + st, carry

    init = jnp.zeros_like(states[:, 0])
    _, prev = lax.scan(step, init, (jnp.moveaxis(states, 1, 0), jnp.moveaxis(chunk_decay, 1, 0)))
    prev = jnp.moveaxis(prev, 0, 1)
    y_off = jnp.einsum('bclgn,bcgrpn->bclgrp', cm, prev) * jnp.exp(acs)[..., None]
    return y_diag + y_off


def mamba2_mixer(h, w_in, conv_w, conv_b, dt_bias, a_log, d_skip, norm_w, w_out):
    b, s, _ = h.shape
    nc = s // CHUNK
    proj = h @ w_in
    z = proj[..., :E_WIDTH]
    xbc = proj[..., E_WIDTH:E_WIDTH + M_CONV_DIM]
    dt_raw = proj[..., E_WIDTH + M_CONV_DIM:]
    xbc = jax.nn.silu(causal_depthwise_conv(xbc, conv_w, conv_b))
    gn = M_GROUPS * M_STATE
    xs = xbc[..., :E_WIDTH].reshape(b, nc, CHUNK, M_GROUPS, M_HEADS_PER_GROUP, M_HEAD_DIM)
    bm = xbc[..., E_WIDTH:E_WIDTH + gn].reshape(b, nc, CHUNK, M_GROUPS, M_STATE)
    cm = xbc[..., E_WIDTH + gn:].reshape(b, nc, CHUNK, M_GROUPS, M_STATE)
    dt = jax.nn.softplus(dt_raw.astype(jnp.float32) + dt_bias.astype(jnp.float32))
    dt = dt.reshape(b, nc, CHUNK, M_GROUPS, M_HEADS_PER_GROUP)
    a = -jnp.exp(a_log.astype(jnp.float32)).reshape(M_GROUPS, M_HEADS_PER_GROUP)
    y = ssd_chunked(xs, dt, a, bm, cm)
    y = y + xs * d_skip.reshape(M_GROUPS, M_HEADS_PER_GROUP, 1)
    y = y.reshape(b, s, E_WIDTH).astype(h.dtype)
    y = gated_group_rmsnorm(y, z, norm_w)
    return y @ w_out


def forgetting_attention_mixer(h, w_in, b_forget, w_out):
    b, s, _ = h.shape
    proj = h @ w_in
    q = proj[..., :E_WIDTH].reshape(b, s, F_HEADS, F_HEAD_DIM)
    k = proj[..., E_WIDTH:2 * E_WIDTH].reshape(b, s, F_HEADS, F_HEAD_DIM)
    v = proj[..., 2 * E_WIDTH:3 * E_WIDTH].reshape(b, s, F_HEADS, F_HEAD_DIM)
    z = proj[..., 3 * E_WIDTH:4 * E_WIDTH]
    f_logit = proj[..., 4 * E_WIDTH:]
    log_f = jax.nn.log_sigmoid((f_logit + b_forget).astype(jnp.float32))
    cum = jnp.cumsum(log_f, axis=1).transpose(0, 2, 1)
    scale = 1.0 / math.sqrt(F_HEAD_DIM)
    outs = []
    for i in range(s // Q_BLOCK):
        q0 = i * Q_BLOCK
        q1 = q0 + Q_BLOCK
        logits = jnp.einsum('bqhd,bkhd->bhqk', q[:, q0:q1], k[:, :q1]).astype(jnp.float32) * scale
        logits = logits + cum[:, :, q0:q1, None] - cum[:, :, None, :q1]
        mask = (q0 + jnp.arange(Q_BLOCK))[:, None] >= jnp.arange(q1)[None, :]
        logits = jnp.where(mask, logits, -jnp.inf)
        p = jax.nn.softmax(logits, axis=-1).astype(v.dtype)
        outs.append(jnp.einsum('bhqk,bkhd->bqhd', p, v[:, :q1]))
    o = jnp.concatenate(outs, axis=1).reshape(b, s, E_WIDTH)
    o = o * jax.nn.silu(z)
    return o @ w_out


def setup_inputs(seed: int = 0) -> dict:
    key = jax.random.key(seed)
    ks = jax.random.split(key, 16)
    f32 = jnp.float32
    x = jax.random.normal(ks[0], (BATCH, SEQ, D_MODEL), f32)
    norm_w = 1.0 + 0.02 * jax.random.normal(ks[1], (DEPTH, D_MODEL), f32)
    nm = N_MAMBA_LAYERS
    m_w_in = jax.random.normal(ks[2], (nm, D_MODEL, M_IN), f32) * D_MODEL ** -0.5
    m_conv_w = jax.random.normal(ks[3], (nm, M_CONV, M_CONV_DIM), f32) * M_CONV ** -0.5
    m_conv_b = 0.02 * jax.random.normal(ks[4], (nm, M_CONV_DIM), f32)
    dt0 = jnp.exp(jax.random.uniform(ks[5], (nm, M_HEADS), f32, math.log(1e-3), math.log(1e-1)))
    m_dt_bias = dt0 + jnp.log(-jnp.expm1(-dt0))
    m_A_log = jnp.log(jax.random.uniform(ks[6], (nm, M_HEADS), f32, 1.0, 16.0))
    m_D = 1.0 + 0.02 * jax.random.normal(ks[7], (nm, M_HEADS), f32)
    m_norm_w = 1.0 + 0.02 * jax.random.normal(ks[8], (nm, E_WIDTH), f32)
    m_w_out = jax.random.normal(ks[9], (nm, E_WIDTH, D_MODEL), f32) * E_WIDTH ** -0.5
    nf = N_FOX_LAYERS
    f_w_in = jax.random.normal(ks[10], (nf, D_MODEL, F_IN), f32) * D_MODEL ** -0.5
    f_b_forget = FORGET_BIAS_INIT + 0.1 * jax.random.normal(ks[11], (nf, F_HEADS), f32)
    f_w_out = jax.random.normal(ks[12], (nf, E_WIDTH, D_MODEL), f32) * E_WIDTH ** -0.5
    final_norm_w = 1.0 + 0.02 * jax.random.normal(ks[13], (D_MODEL,), f32)
    return {"x": x, "norm_w": norm_w, "m_w_in": m_w_in, "m_conv_w": m_conv_w,
            "m_conv_b": m_conv_b, "m_dt_bias": m_dt_bias, "m_A_log": m_A_log,
            "m_D": m_D, "m_norm_w": m_norm_w, "m_w_out": m_w_out,
            "f_w_in": f_w_in, "f_b_forget": f_b_forget, "f_w_out": f_w_out,
            "final_norm_w": final_norm_w}


def reference(x, norm_w, m_w_in, m_conv_w, m_conv_b, m_dt_bias, m_A_log, m_D,
              m_norm_w, m_w_out, f_w_in, f_b_forget, f_w_out, final_norm_w):
    for i in range(DEPTH):
        h = rmsnorm(x, norm_w[i])
        j = i // N_MIXERS
        if i % N_MIXERS == 0:
            y = mamba2_mixer(h, m_w_in[j], m_conv_w[j], m_conv_b[j], m_dt_bias[j],
                             m_A_log[j], m_D[j], m_norm_w[j], m_w_out[j])
        else:
            y = forgetting_attention_mixer(h, f_w_in[j], f_b_forget[j], f_w_out[j])
        x = x + y.astype(x.dtype)
    return rmsnorm(x, final_norm_w)
```

```python
import functools
import math

import jax
import jax.numpy as jnp
from jax import lax
from jax.experimental import pallas as pl
from jax.experimental.pallas import tpu as pltpu

F32 = jnp.float32
BF16 = jnp.bfloat16
EPS = 1e-6
LANES = 128
LOG2E = 1.4426950408889634
NEG_INF = float("-inf")
HIGHEST = lax.Precision.HIGHEST
VMEM_LIMIT = 56 * 1024 * 1024

M_HEAD_DIM = 64
M_GROUPS = 8
M_STATE = 128
M_CONV = 4
SSD_CHUNK = 128
F_HEAD_DIM = 128

_NT = (((1,), (1,)), ((), ()))


def _sigmoid(v):
    return 1.0 / (1.0 + jnp.exp(-v))


def _softplus(v):
    return jnp.maximum(v, 0.0) + jnp.log1p(jnp.exp(-jnp.abs(v)))


def _log_sigmoid(v):
    return jnp.minimum(v, 0.0) - jnp.log1p(jnp.exp(-jnp.abs(v)))


def _norm_matmul_body(x_ref, nw_ref, w_ref, ws_ref, o_ref, os_ref, ost_ref, h_scr,
                      *, tm, tn, head_major, row_chunk):
    j = pl.program_id(1)

    @pl.when(j == 0)
    def _():
        nw = nw_ref[...]

        def norm_rows(r, carry):
            rows = pl.ds(pl.multiple_of(r * row_chunk, row_chunk), row_chunk)
            xv = x_ref[rows, :]
            ms = jnp.mean(xv * xv, axis=-1, keepdims=True)
            h_scr[rows, :] = ((xv * lax.rsqrt(ms + EPS)) * nw).astype(BF16)
            return carry

        lax.fori_loop(0, tm // row_chunk, norm_rows, 0)
        small = jnp.dot(h_scr[...], ws_ref[...], preferred_element_type=F32)
        os_ref[...] = small
        ost_ref[...] = small.T

    res = jnp.dot(h_scr[...], w_ref[...], preferred_element_type=F32).astype(o_ref.dtype)
    if head_major:
        for c in range(tn // LANES):
            o_ref[c] = res[:, c * LANES:(c + 1) * LANES]
    else:
        o_ref[...] = res


def _norm_matmul(x2d, nw, w_main, w_small, *, head_major, tm=1024, tn=512):
    t, d = x2d.shape
    n = w_main.shape[1]
    assert t % tm == 0 and n % tn == 0 and w_small.shape == (d, LANES)
    grid = (t // tm, n // tn)
    if head_major:
        main_shape = jax.ShapeDtypeStruct((n // LANES, t, LANES), BF16)
        main_spec = pl.BlockSpec((tn // LANES, tm, LANES), lambda i, j: (j, i, 0))
    else:
        main_shape = jax.ShapeDtypeStruct((t, n), BF16)
        main_spec = pl.BlockSpec((tm, tn), lambda i, j: (i, j))
    body = functools.partial(_norm_matmul_body, tm=tm, tn=tn, head_major=head_major, row_chunk=64)
    return pl.pallas_call(
        body,
        grid=grid,
        in_specs=[
            pl.BlockSpec((tm, d), lambda i, j: (i, 0)),
            pl.BlockSpec((1, d), lambda i, j: (0, 0)),
            pl.BlockSpec((d, tn), lambda i, j: (0, j)),
            pl.BlockSpec((d, LANES), lambda i, j: (0, 0)),
        ],
        out_specs=[
            main_spec,
            pl.BlockSpec((tm, LANES), lambda i, j: (i, 0)),
            pl.BlockSpec((LANES, tm), lambda i, j: (0, i)),
        ],
        out_shape=[
            main_shape,
            jax.ShapeDtypeStruct((t, LANES), F32),
            jax.ShapeDtypeStruct((LANES, t), F32),
        ],
        scratch_shapes=[pltpu.VMEM((tm, d), BF16)],
        compiler_params=pltpu.CompilerParams(
            dimension_semantics=("parallel", "arbitrary"), vmem_limit_bytes=VMEM_LIMIT),
        name="norm_in_proj_hm" if head_major else "norm_in_proj",
    )(x2d, nw.reshape(1, d), w_main, w_small)


def _ssd_body(z_ref, x_ref, b_ref, c_ref, dt_ref, dtt_ref, cw_ref, cbias_ref,
              dtb_r_ref, dtb_c_ref, alog_r_ref, alog_c_ref, dskip_ref, nw_ref,
              o_ref,
              ext, xc, st, ysc, acs_s, acst_s, dtt_s, wt_s, eacs_s, cd_s,
              *, e_width, gn):
    q = SSD_CHUNK
    conv_dim = e_width + 2 * gn
    cidx = pl.program_id(1)

    @pl.when(cidx == 0)
    def _():
        ext[0:8, :] = jnp.zeros((8, conv_dim), F32)
        st[...] = jnp.zeros(st.shape, F32)

    ext[pl.ds(8, q), 0:e_width] = x_ref[...].astype(F32)
    ext[pl.ds(8, q), e_width:e_width + gn] = b_ref[...].astype(F32)
    ext[pl.ds(8, q), e_width + gn:conv_dim] = c_ref[...].astype(F32)

    lane_chunk = 512

    def conv_chunk(i, carry):
        lanes = pl.ds(pl.multiple_of(i * lane_chunk, lane_chunk), lane_chunk)
        acc = cbias_ref[:, lanes] + cw_ref[0:1, lanes] * ext[pl.ds(8 - (M_CONV - 1), q), lanes]
        for k in range(1, M_CONV):
            acc = acc + cw_ref[k:k + 1, lanes] * ext[pl.ds(8 - (M_CONV - 1) + k, q), lanes]
        xc[:, lanes] = acc * _sigmoid(acc)
        return carry

    lax.fori_loop(0, conv_dim // lane_chunk, conv_chunk, 0)
    ext[0:8, :] = ext[pl.ds(q, 8), :]

    li = lax.broadcasted_iota(jnp.int32, (q, q), 0)
    lj = lax.broadcasted_iota(jnp.int32, (q, q), 1)
    causal = lj <= li
    tril = causal.astype(F32)
    triu = (li <= lj).astype(F32)

    dt = _softplus(dt_ref[...] + dtb_r_ref[...])
    dtt = _softplus(dtt_ref[...] + dtb_c_ref[...])
    a_r = -jnp.exp(alog_r_ref[...])
    a_c = -jnp.exp(alog_c_ref[...])
    acs = jnp.dot(tril, dt * a_r, preferred_element_type=F32, precision=HIGHEST)
    acst = jnp.dot(dtt * a_c, triu, preferred_element_type=F32, precision=HIGHEST)
    acs_s[...] = acs
    acst_s[...] = acst
    dtt_s[...] = dtt
    wt_s[...] = jnp.exp(acst[:, q - 1:q] - acst) * dtt
    eacs_s[...] = jnp.exp(acs)
    cd_s[...] = jnp.exp(acs[q - 1:q, :])

    lane = lax.broadcasted_iota(jnp.int32, (q, LANES), 1)
    lo_half = lane < M_HEAD_DIM
    lane_row = lax.broadcasted_iota(jnp.int32, (1, LANES), 1)
    lo_half_row = lane_row < M_HEAD_DIM
    hpg = e_width // M_HEAD_DIM // M_GROUPS
    gw = hpg * M_HEAD_DIM

    for g in range(M_GROUPS):
        bg = xc[:, e_width + g * M_STATE:e_width + (g + 1) * M_STATE]
        cg = xc[:, e_width + gn + g * M_STATE:e_width + gn + (g + 1) * M_STATE]
        bgb = bg.astype(BF16)
        cgb = cg.astype(BF16)
        cb = lax.dot_general(cgb, bgb, _NT, preferred_element_type=F32)
        bgt = bg.T
        cs = jnp.dot(cgb, st[:, g * gw:(g + 1) * gw].astype(BF16),
                     preferred_element_type=F32)
        for pr in range(hpg // 2):
            h0 = g * hpg + 2 * pr
            c0 = g * gw + pr * LANES
            xs_f = xc[:, c0:c0 + LANES]
            xs_b = xs_f.astype(BF16)
            zero = jnp.zeros_like(xs_b)
            rhs = jnp.concatenate([jnp.where(lo_half, xs_b, zero),
                                   jnp.where(lo_half, zero, xs_b)], axis=0)
            m_parts = []
            w_parts = []
            for hh in (h0, h0 + 1):
                seg = acs_s[:, hh:hh + 1] - acst_s[hh:hh + 1, :]
                lm = jnp.exp(jnp.where(causal, seg, NEG_INF))
                m_parts.append((cb * lm * dtt_s[hh:hh + 1, :]).astype(BF16))
                w_parts.append((bgt * wt_s[hh:hh + 1, :]).astype(BF16))
            lhs = jnp.concatenate([jnp.concatenate(m_parts, axis=1),
                                   jnp.concatenate(w_parts, axis=1)], axis=0)
            res = jnp.dot(lhs, rhs, preferred_element_type=F32)
            y_diag = res[0:q]
            s_new = res[q:2 * q]
            e_pair = jnp.where(lo_half, eacs_s[:, h0:h0 + 1], eacs_s[:, h0 + 1:h0 + 2])
            y_pair = (y_diag + e_pair * cs[:, pr * LANES:(pr + 1) * LANES]
                      + xs_f * dskip_ref[:, c0:c0 + LANES])
            cd_pair = jnp.where(lo_half_row, cd_s[:, h0:h0 + 1], cd_s[:, h0 + 1:h0 + 2])
            st[:, c0:c0 + LANES] = st[:, c0:c0 + LANES] * cd_pair + s_new
            ysc[:, c0:c0 + LANES] = y_pair
        zg = z_ref[:, g * gw:(g + 1) * gw].astype(F32)
        gt = ysc[:, g * gw:(g + 1) * gw] * (zg * _sigmoid(zg))
        ms = jnp.mean(gt * gt, axis=-1, keepdims=True)
        o_ref[:, g * gw:(g + 1) * gw] = (
            (gt * lax.rsqrt(ms + EPS)) * nw_ref[:, g * gw:(g + 1) * gw]).astype(BF16)


def _ssd(zx, dt_raw, dtt_raw, conv_w, conv_b, dt_bias, a_log, d_skip, norm_w, *, batch, seq, e_width):
    t = batch * seq
    gn = M_GROUPS * M_STATE
    conv_dim = e_width + 2 * gn
    q = SSD_CHUNK
    nc = seq // q
    n_heads = e_width // M_HEAD_DIM
    assert zx.shape == (t, e_width + conv_dim) and seq % q == 0
    assert e_width % gn == 0 and gn % LANES == 0

    def pad_row(v):
        return jnp.pad(v.astype(F32), (0, LANES - n_heads)).reshape(1, LANES)

    dtb_r = pad_row(dt_bias)
    alog_r = pad_row(a_log)
    dskip = jnp.repeat(d_skip.astype(F32), M_HEAD_DIM).reshape(1, e_width)
    rowblk = lambda b, c: b * nc + c
    const = lambda b, c: (0, 0)
    body = functools.partial(_ssd_body, e_width=e_width, gn=gn)
    return pl.pallas_call(
        body,
        grid=(batch, nc),
        in_specs=[
            pl.BlockSpec((q, e_width), lambda b, c: (rowblk(b, c), 0)),
            pl.BlockSpec((q, e_width), lambda b, c: (rowblk(b, c), 1)),
            pl.BlockSpec((q, gn), lambda b, c: (rowblk(b, c), 2 * e_width // gn)),
            pl.BlockSpec((q, gn), lambda b, c: (rowblk(b, c), 2 * e_width // gn + 1)),
            pl.BlockSpec((q, LANES), lambda b, c: (rowblk(b, c), 0)),
            pl.BlockSpec((LANES, q), lambda b, c: (0, rowblk(b, c))),
            pl.BlockSpec((M_CONV, conv_dim), const),
            pl.BlockSpec((1, conv_dim), const),
            pl.BlockSpec((1, LANES), const),
            pl.BlockSpec((LANES, 1), const),
            pl.BlockSpec((1, LANES), const),
            pl.BlockSpec((LANES, 1), const),
            pl.BlockSpec((1, e_width), const),
            pl.BlockSpec((1, e_width), const),
        ],
        out_specs=pl.BlockSpec((q, e_width), lambda b, c: (rowblk(b, c), 0)),
        out_shape=jax.ShapeDtypeStruct((t, e_width), BF16),
        scratch_shapes=[
            pltpu.VMEM((q + 8, conv_dim), F32),
            pltpu.VMEM((q, conv_dim), F32),
            pltpu.VMEM((M_STATE, e_width), F32),
            pltpu.VMEM((q, e_width), F32),
            pltpu.VMEM((q, LANES), F32),
            pltpu.VMEM((LANES, q), F32),
            pltpu.VMEM((LANES, q), F32),
            pltpu.VMEM((LANES, q), F32),
            pltpu.VMEM((q, LANES), F32),
            pltpu.VMEM((1, LANES), F32),
        ],
        compiler_params=pltpu.CompilerParams(
            dimension_semantics=("parallel", "arbitrary"), vmem_limit_bytes=VMEM_LIMIT),
        name="ssd_mixer",
    )(zx, zx, zx, zx, dt_raw, dtt_raw, conv_w.astype(F32), conv_b.astype(F32).reshape(1, conv_dim),
      dtb_r, dtb_r.reshape(LANES, 1), alog_r, alog_r.reshape(LANES, 1), dskip,
      norm_w.astype(F32).reshape(1, e_width))


def _out_proj_body(g_ref, w_ref, x_ref, fw_ref, o_ref, acc, *, final_norm):
    k = pl.program_id(1)

    @pl.when(k == 0)
    def _():
        acc[...] = jnp.zeros(acc.shape, F32)

    acc[...] += jnp.dot(g_ref[...], w_ref[...], preferred_element_type=F32)

    @pl.when(k == pl.num_programs(1) - 1)
    def _():
        r = x_ref[...] + acc[...]
        if final_norm:
            ms = jnp.mean(r * r, axis=-1, keepdims=True)
            r = (r * lax.rsqrt(ms + EPS)) * fw_ref[...]
        o_ref[...] = r


def _out_proj(g, w, x2d, final_w, *, tm=512, tk=512):
    t, kdim = g.shape
    d = w.shape[1]
    final_norm = final_w is not None
    fw = (final_w if final_norm else jnp.ones((d,), F32)).astype(F32).reshape(1, d)
    body = functools.partial(_out_proj_body, final_norm=final_norm)
    return pl.pallas_call(
        body,
        grid=(t // tm, kdim // tk),
        in_specs=[
            pl.BlockSpec((tm, tk), lambda i, k: (i, k)),
            pl.BlockSpec((tk, d), lambda i, k: (k, 0)),
            pl.BlockSpec((tm, d), lambda i, k: (i, 0)),
            pl.BlockSpec((1, d), lambda i, k: (0, 0)),
        ],
        out_specs=pl.BlockSpec((tm, d), lambda i, k: (i, 0)),
        out_shape=jax.ShapeDtypeStruct((t, d), F32),
        scratch_shapes=[pltpu.VMEM((tm, d), F32)],
        compiler_params=pltpu.CompilerParams(
            dimension_semantics=("parallel", "arbitrary"), vmem_limit_bytes=VMEM_LIMIT),
        name="out_proj_final" if final_norm else "out_proj",
    )(g, w, x2d, fw)


def _forget_cumsum_body(f_ref, ft_ref, b_r_ref, b_c_ref, cc_ref, cr_ref, carry_r, carry_c, *, blk):
    i = pl.program_id(1)

    @pl.when(i == 0)
    def _():
        carry_r[...] = jnp.zeros(carry_r.shape, F32)
        carry_c[...] = jnp.zeros(carry_c.shape, F32)

    li = lax.broadcasted_iota(jnp.int32, (blk, blk), 0)
    lj = lax.broadcasted_iota(jnp.int32, (blk, blk), 1)
    tril = (lj <= li).astype(F32)
    triu = (li <= lj).astype(F32)
    lf = _log_sigmoid(f_ref[...] + b_r_ref[...])
    cum = jnp.dot(tril, lf, preferred_element_type=F32, precision=HIGHEST) + carry_r[...]
    cc_ref[...] = cum
    carry_r[...] = cum[blk - 1:blk, :]
    lft = _log_sigmoid(ft_ref[...] + b_c_ref[...])
    cumt = jnp.dot(lft, triu, preferred_element_type=F32, precision=HIGHEST) + carry_c[...]
    cr_ref[0] = cumt
    carry_c[...] = cumt[:, blk - 1:blk]


def _forget_cumsum(f, ft, b_forget, *, batch, seq, blk=512):
    t = batch * seq
    nb = seq // blk
    n_heads = b_forget.shape[0]
    b_r = jnp.pad(b_forget.astype(F32), (0, LANES - n_heads)).reshape(1, LANES)
    body = functools.partial(_forget_cumsum_body, blk=blk)
    return pl.pallas_call(
        body,
        grid=(batch, nb),
        in_specs=[
            pl.BlockSpec((blk, LANES), lambda b, i: (b * nb + i, 0)),
            pl.BlockSpec((LANES, blk), lambda b, i: (0, b * nb + i)),
            pl.BlockSpec((1, LANES), lambda b, i: (0, 0)),
            pl.BlockSpec((LANES, 1), lambda b, i: (0, 0)),
        ],
        out_specs=[
            pl.BlockSpec((blk, LANES), lambda b, i: (b * nb + i, 0)),
            pl.BlockSpec((1, LANES, blk), lambda b, i: (b, 0, i)),
        ],
        out_shape=[
            jax.ShapeDtypeStruct((t, LANES), F32),
            jax.ShapeDtypeStruct((batch, LANES, seq), F32),
        ],
        scratch_shapes=[pltpu.VMEM((1, LANES), F32), pltpu.VMEM((LANES, 1), F32)],
        compiler_params=pltpu.CompilerParams(
            dimension_semantics=("parallel", "arbitrary"), vmem_limit_bytes=VMEM_LIMIT),
        name="forget_cumsum",
    )(f, ft, b_r, b_r.reshape(LANES, 1))


def _fox_attn_body(q_ref, k_ref, v_ref, z_ref, cc_ref, cr_ref, o_ref,
                   m_scr, l_scr, acc_scr, *, seq, tq, tk, scale2):
    h = pl.program_id(1)
    nq = seq // tq
    assert tq == tk
    lane = lax.broadcasted_iota(jnp.int32, (tq, LANES), 1)
    row_i = lax.broadcasted_iota(jnp.int32, (tq, tk), 0)
    col_i = lax.broadcasted_iota(jnp.int32, (tq, tk), 1)
    diag_mask = row_i >= col_i
    hrow = pl.ds(h % 8, 1)

    def q_tile(qi, carry):
        qrows = pl.ds(pl.multiple_of(qi * tq, tq), tq)
        qv = q_ref[0, qrows, :]
        cq = jnp.sum(jnp.where(lane == h, cc_ref[qrows, :], 0.0), axis=-1, keepdims=True) * LOG2E
        m_scr[...] = jnp.full(m_scr.shape, NEG_INF, F32)
        l_scr[...] = jnp.zeros(l_scr.shape, F32)
        acc_scr[...] = jnp.zeros(acc_scr.shape, F32)

        def kv_block(ki, masked):
            krows = pl.ds(pl.multiple_of(ki * tk, tk), tk)
            kv = k_ref[0, krows, :]
            vv = v_ref[0, krows, :]
            s = lax.dot_general(qv, kv, _NT, preferred_element_type=F32)
            ck = cr_ref[0, hrow, krows] * LOG2E
            tt = s * scale2 - ck
            if masked:
                tt = jnp.where(diag_mask, tt, NEG_INF)
            m_prev = m_scr[...]
            m_new = jnp.maximum(m_prev, jnp.max(tt, axis=-1, keepdims=True) + cq)
            alpha = jnp.exp2(m_prev - m_new)
            p = jnp.exp2(tt + (cq - m_new))
            l_scr[...] = alpha * l_scr[...] + jnp.sum(p, axis=-1, keepdims=True)
            acc_scr[...] = alpha * acc_scr[...] + jnp.dot(p.astype(BF16), vv,
                                                          preferred_element_type=F32)
            m_scr[...] = m_new

        def off_diag(ki, c):
            kv_block(ki, False)
            return c

        lax.fori_loop(0, qi, off_diag, 0)
        kv_block(qi, True)
        zv = z_ref[0, qrows, :].astype(F32)
        o_ref[qrows, :] = ((acc_scr[...] / l_scr[...]) * (zv * _sigmoid(zv))).astype(BF16)
        return carry

    lax.fori_loop(0, nq, q_tile, 0)


def _fox_attention(p_hm, cum_c, cum_r, *, batch, seq, n_heads, tq=512, tk=512):
    t = batch * seq
    e_width = n_heads * F_HEAD_DIM
    scale2 = LOG2E / math.sqrt(F_HEAD_DIM)
    body = functools.partial(_fox_attn_body, seq=seq, tq=tq, tk=tk, scale2=scale2)
    hm = lambda off: pl.BlockSpec((1, seq, F_HEAD_DIM), lambda b, h: (off + h, b, 0))
    return pl.pallas_call(
        body,
        grid=(batch, n_heads),
        in_specs=[
            hm(0), hm(n_heads), hm(2 * n_heads), hm(3 * n_heads),
            pl.BlockSpec((seq, LANES), lambda b, h: (b, 0)),
            pl.BlockSpec((1, 8, seq), lambda b, h: (b, h // 8, 0)),
        ],
        out_specs=pl.BlockSpec((seq, F_HEAD_DIM), lambda b, h: (b, h)),
        out_shape=jax.ShapeDtypeStruct((t, e_width), BF16),
        scratch_shapes=[
            pltpu.VMEM((tq, 1), F32),
            pltpu.VMEM((tq, 1), F32),
            pltpu.VMEM((tq, F_HEAD_DIM), F32),
        ],
        compiler_params=pltpu.CompilerParams(
            dimension_semantics=("parallel", "arbitrary"), vmem_limit_bytes=VMEM_LIMIT),
        name="fox_attention",
    )(p_hm, p_hm, p_hm, p_hm, cum_c, cum_r)


def kernel(x, norm_w, m_w_in, m_conv_w, m_conv_b, m_dt_bias, m_A_log, m_D, m_norm_w, m_w_out,
           f_w_in, f_b_forget, f_w_out, final_norm_w):
    batch, seq, d_model = x.shape
    t = batch * seq
    e_width = m_w_out.shape[1]
    m_heads = m_dt_bias.shape[1]
    f_heads = f_b_forget.shape[1]
    conv_dim = m_conv_w.shape[2]
    assert norm_w.shape[0] == 2 and m_w_in.shape[0] == 1 and f_w_in.shape[0] == 1
    assert e_width == m_heads * M_HEAD_DIM == f_heads * F_HEAD_DIM

    xf = x.reshape(t, d_model).astype(F32)

    w_in = m_w_in[0]
    n_main = e_width + conv_dim
    w_main = w_in[:, :n_main].astype(BF16)
    w_dt = jnp.pad(w_in[:, n_main:], ((0, 0), (0, LANES - m_heads))).astype(BF16)
    zx, dt_raw, dtt_raw = _norm_matmul(xf, norm_w[0], w_main, w_dt, head_major=False)
    yg = _ssd(zx, dt_raw, dtt_raw, m_conv_w[0], m_conv_b[0], m_dt_bias[0], m_A_log[0], m_D[0],
              m_norm_w[0], batch=batch, seq=seq, e_width=e_width)
    x1 = _out_proj(yg, m_w_out[0].astype(BF16), xf, None)

    fw_in = f_w_in[0]
    fw_main = fw_in[:, :4 * e_width].astype(BF16)
    fw_f = jnp.pad(fw_in[:, 4 * e_width:], ((0, 0), (0, LANES - f_heads))).astype(BF16)
    p_hm, f_raw, ft_raw = _norm_matmul(x1, norm_w[1], fw_main, fw_f, head_major=True)
    cum_c, cum_r = _forget_cumsum(f_raw, ft_raw, f_b_forget[0], batch=batch, seq=seq)
    og = _fox_attention(p_hm, cum_c, cum_r, batch=batch, seq=seq, n_heads=f_heads)
    out = _out_proj(og, f_w_out[0].astype(BF16), x1, final_norm_w)
    return out.reshape(batch, seq, d_model)
```

```python
import functools
import math

import jax
import jax.numpy as jnp
from jax import lax
from jax.experimental import pallas as pl
from jax.experimental.pallas import tpu as pltpu

F32 = jnp.float32
BF16 = jnp.bfloat16
EPS = 1e-6
LANES = 128
LOG2E = 1.4426950408889634
NEG_INF = float("-inf")
HIGHEST = lax.Precision.HIGHEST
VMEM_LIMIT = 56 * 1024 * 1024

M_HEAD_DIM = 64
M_GROUPS = 8
M_STATE = 128
M_CONV = 4
SSD_CHUNK = 128
F_HEAD_DIM = 128

_NT = (((1,), (1,)), ((), ()))


def _sigmoid(v):
    return 1.0 / (1.0 + jnp.exp(-v))


def _softplus(v):
    return jnp.maximum(v, 0.0) + jnp.log1p(jnp.exp(-jnp.abs(v)))


def _log_sigmoid(v):
    return jnp.minimum(v, 0.0) - jnp.log1p(jnp.exp(-jnp.abs(v)))


def _norm_matmul_body(x_ref, nw_ref, w_ref, ws_ref, o_ref, os_ref, *rest,
                      tm, tn, head_major, row_chunk, with_transposed, lead_tiles, lead_scale):
    if with_transposed:
        ost_ref, h_scr = rest
    else:
        (h_scr,) = rest
    j = pl.program_id(1)

    @pl.when(j == 0)
    def _():
        nw = nw_ref[...]

        def norm_rows(r, carry):
            rows = pl.ds(pl.multiple_of(r * row_chunk, row_chunk), row_chunk)
            xv = x_ref[rows, :]
            ms = jnp.mean(xv * xv, axis=-1, keepdims=True)
            h_scr[rows, :] = ((xv * lax.rsqrt(ms + EPS)) * nw).astype(BF16)
            return carry

        lax.fori_loop(0, tm // row_chunk, norm_rows, 0)
        small = jnp.dot(h_scr[...], ws_ref[...], preferred_element_type=F32)
        os_ref[...] = small
        if with_transposed:
            ost_ref[...] = small.T

    res = jnp.dot(h_scr[...], w_ref[...], preferred_element_type=F32)
    if lead_tiles:
        res = res * jnp.where(j < lead_tiles, jnp.float32(lead_scale), jnp.float32(1.0))
    res = res.astype(o_ref.dtype)
    if head_major:
        for c in range(tn // LANES):
            o_ref[c] = res[:, c * LANES:(c + 1) * LANES]
    else:
        o_ref[...] = res


def _norm_matmul(x2d, nw, w_main, w_small, *, head_major, with_transposed, lead_cols=0, lead_scale=1.0,
                 tm=1024, tn=512):
    t, d = x2d.shape
    n = w_main.shape[1]
    assert t % tm == 0 and n % tn == 0 and w_small.shape == (d, LANES) and lead_cols % tn == 0
    grid = (t // tm, n // tn)
    if head_major:
        main_shape = jax.ShapeDtypeStruct((n // LANES, t, LANES), BF16)
        main_spec = pl.BlockSpec((tn // LANES, tm, LANES), lambda i, j: (j, i, 0))
    else:
        main_shape = jax.ShapeDtypeStruct((t, n), BF16)
        main_spec = pl.BlockSpec((tm, tn), lambda i, j: (i, j))
    out_specs = [main_spec, pl.BlockSpec((tm, LANES), lambda i, j: (i, 0))]
    out_shape = [main_shape, jax.ShapeDtypeStruct((t, LANES), F32)]
    if with_transposed:
        out_specs.append(pl.BlockSpec((LANES, tm), lambda i, j: (0, i)))
        out_shape.append(jax.ShapeDtypeStruct((LANES, t), F32))
    body = functools.partial(_norm_matmul_body, tm=tm, tn=tn, head_major=head_major, row_chunk=64,
                             with_transposed=with_transposed, lead_tiles=lead_cols // tn,
                             lead_scale=lead_scale)
    return pl.pallas_call(
        body,
        grid=grid,
        in_specs=[
            pl.BlockSpec((tm, d), lambda i, j: (i, 0)),
            pl.BlockSpec((1, d), lambda i, j: (0, 0)),
            pl.BlockSpec((d, tn), lambda i, j: (0, j)),
            pl.BlockSpec((d, LANES), lambda i, j: (0, 0)),
        ],
        out_specs=out_specs,
        out_shape=out_shape,
        scratch_shapes=[pltpu.VMEM((tm, d), BF16)],
        compiler_params=pltpu.CompilerParams(
            dimension_semantics=("parallel", "arbitrary"), vmem_limit_bytes=VMEM_LIMIT),
        name="norm_in_proj_hm" if head_major else "norm_in_proj",
    )(x2d, nw.reshape(1, d), w_main, w_small)


def _ssd_body(z_ref, x_ref, b_ref, c_ref, dt_ref, dtt_ref, cw_ref, cbias_ref,
              dtb_r_ref, dtb_c_ref, alog_r_ref, alog_c_ref, dskip_ref, nw_ref,
              o_ref,
              ext, xc, st, ysc, acs_s, acst_s, dtt_s, wt_s, eacs_s, cd_s,
              *, e_width, gn):
    q = SSD_CHUNK
    conv_dim = e_width + 2 * gn
    cidx = pl.program_id(1)

    @pl.when(cidx == 0)
    def _():
        ext[0:8, :] = jnp.zeros((8, conv_dim), F32)
        st[...] = jnp.zeros(st.shape, F32)

    ext[pl.ds(8, q), 0:e_width] = x_ref[...].astype(F32)
    ext[pl.ds(8, q), e_width:e_width + gn] = b_ref[...].astype(F32)
    ext[pl.ds(8, q), e_width + gn:conv_dim] = c_ref[...].astype(F32)

    lane_chunk = 512

    def conv_chunk(i, carry):
        lanes = pl.ds(pl.multiple_of(i * lane_chunk, lane_chunk), lane_chunk)
        acc = cbias_ref[:, lanes] + cw_ref[0:1, lanes] * ext[pl.ds(8 - (M_CONV - 1), q), lanes]
        for k in range(1, M_CONV):
            acc = acc + cw_ref[k:k + 1, lanes] * ext[pl.ds(8 - (M_CONV - 1) + k, q), lanes]
        xc[:, lanes] = acc * _sigmoid(acc)
        return carry

    lax.fori_loop(0, conv_dim // lane_chunk, conv_chunk, 0)
    ext[0:8, :] = ext[pl.ds(q, 8), :]

    li = lax.broadcasted_iota(jnp.int32, (q, q), 0)
    lj = lax.broadcasted_iota(jnp.int32, (q, q), 1)
    causal = lj <= li
    tril = causal.astype(F32)
    triu = (li <= lj).astype(F32)

    dt = _softplus(dt_ref[...] + dtb_r_ref[...])
    dtt = _softplus(dtt_ref[...] + dtb_c_ref[...])
    a_r = -jnp.exp(alog_r_ref[...])
    a_c = -jnp.exp(alog_c_ref[...])
    acs = jnp.dot(tril, dt * a_r, preferred_element_type=F32, precision=HIGHEST)
    acst = jnp.dot(dtt * a_c, triu, preferred_element_type=F32, precision=HIGHEST)
    acs_s[...] = acs
    acst_s[...] = acst
    dtt_s[...] = dtt
    wt_s[...] = jnp.exp(acst[:, q - 1:q] - acst) * dtt
    eacs_s[...] = jnp.exp(acs)
    cd_s[...] = jnp.exp(acs[q - 1:q, :])

    lane = lax.broadcasted_iota(jnp.int32, (q, LANES), 1)
    lo_half = lane < M_HEAD_DIM
    lane_row = lax.broadcasted_iota(jnp.int32, (1, LANES), 1)
    lo_half_row = lane_row < M_HEAD_DIM
    hpg = e_width // M_HEAD_DIM // M_GROUPS
    gw = hpg * M_HEAD_DIM

    for g in range(M_GROUPS):
        bg = xc[:, e_width + g * M_STATE:e_width + (g + 1) * M_STATE]
        cg = xc[:, e_width + gn + g * M_STATE:e_width + gn + (g + 1) * M_STATE]
        bgb = bg.astype(BF16)
        cgb = cg.astype(BF16)
        cb = lax.dot_general(cgb, bgb, _NT, preferred_element_type=F32)
        bgt = bg.T
        cs = jnp.dot(cgb, st[:, g * gw:(g + 1) * gw].astype(BF16),
                     preferred_element_type=F32)
        for pr in range(hpg // 2):
            h0 = g * hpg + 2 * pr
            c0 = g * gw + pr * LANES
            xs_f = xc[:, c0:c0 + LANES]
            xs_b = xs_f.astype(BF16)
            zero = jnp.zeros_like(xs_b)
            rhs = jnp.concatenate([jnp.where(lo_half, xs_b, zero),
                                   jnp.where(lo_half, zero, xs_b)], axis=0)
            m_parts = []
            w_parts = []
            for hh in (h0, h0 + 1):
                seg = acs_s[:, hh:hh + 1] - acst_s[hh:hh + 1, :]
                lm = jnp.exp(jnp.where(causal, seg, NEG_INF))
                m_parts.append((cb * lm * dtt_s[hh:hh + 1, :]).astype(BF16))
                w_parts.append((bgt * wt_s[hh:hh + 1, :]).astype(BF16))
            lhs = jnp.concatenate([jnp.concatenate(m_parts, axis=1),
                                   jnp.concatenate(w_parts, axis=1)], axis=0)
            res = jnp.dot(lhs, rhs, preferred_element_type=F32)
            y_diag = res[0:q]
            s_new = res[q:2 * q]
            e_pair = jnp.where(lo_half, eacs_s[:, h0:h0 + 1], eacs_s[:, h0 + 1:h0 + 2])
            y_pair = (y_diag + e_pair * cs[:, pr * LANES:(pr + 1) * LANES]
                      + xs_f * dskip_ref[:, c0:c0 + LANES])
            cd_pair = jnp.where(lo_half_row, cd_s[:, h0:h0 + 1], cd_s[:, h0 + 1:h0 + 2])
            st[:, c0:c0 + LANES] = st[:, c0:c0 + LANES] * cd_pair + s_new
            ysc[:, c0:c0 + LANES] = y_pair
        zg = z_ref[:, g * gw:(g + 1) * gw].astype(F32)
        gt = ysc[:, g * gw:(g + 1) * gw] * (zg * _sigmoid(zg))
        ms = jnp.mean(gt * gt, axis=-1, keepdims=True)
        o_ref[:, g * gw:(g + 1) * gw] = (
            (gt * lax.rsqrt(ms + EPS)) * nw_ref[:, g * gw:(g + 1) * gw]).astype(BF16)


def _ssd(zx, dt_raw, dtt_raw, conv_w, conv_b, dt_bias, a_log, d_skip, norm_w, *, batch, seq, e_width):
    t = batch * seq
    gn = M_GROUPS * M_STATE
    conv_dim = e_width + 2 * gn
    q = SSD_CHUNK
    nc = seq // q
    n_heads = e_width // M_HEAD_DIM
    assert zx.shape == (t, e_width + conv_dim) and seq % q == 0
    assert e_width % gn == 0 and gn % LANES == 0

    def pad_row(v):
        return jnp.pad(v.astype(F32), (0, LANES - n_heads)).reshape(1, LANES)

    dtb_r = pad_row(dt_bias)
    alog_r = pad_row(a_log)
    dskip = jnp.repeat(d_skip.astype(F32), M_HEAD_DIM).reshape(1, e_width)
    rowblk = lambda b, c: b * nc + c
    const = lambda b, c: (0, 0)
    body = functools.partial(_ssd_body, e_width=e_width, gn=gn)
    return pl.pallas_call(
        body,
        grid=(batch, nc),
        in_specs=[
            pl.BlockSpec((q, e_width), lambda b, c: (rowblk(b, c), 0)),
            pl.BlockSpec((q, e_width), lambda b, c: (rowblk(b, c), 1)),
            pl.BlockSpec((q, gn), lambda b, c: (rowblk(b, c), 2 * e_width // gn)),
            pl.BlockSpec((q, gn), lambda b, c: (rowblk(b, c), 2 * e_width // gn + 1)),
            pl.BlockSpec((q, LANES), lambda b, c: (rowblk(b, c), 0)),
            pl.BlockSpec((LANES, q), lambda b, c: (0, rowblk(b, c))),
            pl.BlockSpec((M_CONV, conv_dim), const),
            pl.BlockSpec((1, conv_dim), const),
            pl.BlockSpec((1, LANES), const),
            pl.BlockSpec((LANES, 1), const),
            pl.BlockSpec((1, LANES), const),
            pl.BlockSpec((LANES, 1), const),
            pl.BlockSpec((1, e_width), const),
            pl.BlockSpec((1, e_width), const),
        ],
        out_specs=pl.BlockSpec((q, e_width), lambda b, c: (rowblk(b, c), 0)),
        out_shape=jax.ShapeDtypeStruct((t, e_width), BF16),
        scratch_shapes=[
            pltpu.VMEM((q + 8, conv_dim), F32),
            pltpu.VMEM((q, conv_dim), F32),
            pltpu.VMEM((M_STATE, e_width), F32),
            pltpu.VMEM((q, e_width), F32),
            pltpu.VMEM((q, LANES), F32),
            pltpu.VMEM((LANES, q), F32),
            pltpu.VMEM((LANES, q), F32),
            pltpu.VMEM((LANES, q), F32),
            pltpu.VMEM((q, LANES), F32),
            pltpu.VMEM((1, LANES), F32),
        ],
        compiler_params=pltpu.CompilerParams(
            dimension_semantics=("parallel", "arbitrary"), vmem_limit_bytes=VMEM_LIMIT),
        name="ssd_mixer",
    )(zx, zx, zx, zx, dt_raw, dtt_raw, conv_w.astype(F32), conv_b.astype(F32).reshape(1, conv_dim),
      dtb_r, dtb_r.reshape(LANES, 1), alog_r, alog_r.reshape(LANES, 1), dskip,
      norm_w.astype(F32).reshape(1, e_width))


def _out_proj_body(g_ref, w_ref, x_ref, fw_ref, o_ref, acc, *, final_norm):
    k = pl.program_id(1)

    @pl.when(k == 0)
    def _():
        acc[...] = jnp.zeros(acc.shape, F32)

    acc[...] += jnp.dot(g_ref[...], w_ref[...], preferred_element_type=F32)

    @pl.when(k == pl.num_programs(1) - 1)
    def _():
        r = x_ref[...] + acc[...]
        if final_norm:
            ms = jnp.mean(r * r, axis=-1, keepdims=True)
            r = (r * lax.rsqrt(ms + EPS)) * fw_ref[...]
        o_ref[...] = r


def _out_proj(g, w, x2d, final_w, *, tm=512, tk=512):
    t, kdim = g.shape
    d = w.shape[1]
    final_norm = final_w is not None
    fw = (final_w if final_norm else jnp.ones((d,), F32)).astype(F32).reshape(1, d)
    body = functools.partial(_out_proj_body, final_norm=final_norm)
    return pl.pallas_call(
        body,
        grid=(t // tm, kdim // tk),
        in_specs=[
            pl.BlockSpec((tm, tk), lambda i, k: (i, k)),
            pl.BlockSpec((tk, d), lambda i, k: (k, 0)),
            pl.BlockSpec((tm, d), lambda i, k: (i, 0)),
            pl.BlockSpec((1, d), lambda i, k: (0, 0)),
        ],
        out_specs=pl.BlockSpec((tm, d), lambda i, k: (i, 0)),
        out_shape=jax.ShapeDtypeStruct((t, d), F32),
        scratch_shapes=[pltpu.VMEM((tm, d), F32)],
        compiler_params=pltpu.CompilerParams(
            dimension_semantics=("parallel", "arbitrary"), vmem_limit_bytes=VMEM_LIMIT),
        name="out_proj_final" if final_norm else "out_proj",
    )(g, w, x2d, fw)


GATE_COLS = 8
GATE_HEADS_PER_BLOCK = LANES // GATE_COLS


def _forget_cumsum_body(f_ref, b_r_ref, qx_ref, kx_ref, carry_r, *, blk, n_heads):
    i = pl.program_id(1)

    @pl.when(i == 0)
    def _():
        carry_r[...] = jnp.zeros(carry_r.shape, F32)

    li = lax.broadcasted_iota(jnp.int32, (blk, blk), 0)
    lj = lax.broadcasted_iota(jnp.int32, (blk, blk), 1)
    tril = (lj <= li).astype(F32)
    lf = _log_sigmoid(f_ref[...] + b_r_ref[...])
    cum = jnp.dot(tril, lf, preferred_element_type=F32, precision=HIGHEST) + carry_r[...]
    carry_r[...] = cum[blk - 1:blk, :]
    c2 = cum * LOG2E
    hi = c2.astype(BF16)
    r1 = c2 - hi.astype(F32)
    mid = r1.astype(BF16)
    lo = (r1 - mid.astype(F32)).astype(BF16)
    parts = jnp.concatenate([hi, mid, lo], axis=1)
    wide = 2 * LANES
    row = lax.broadcasted_iota(jnp.int32, (3 * LANES, wide), 0)
    col = lax.broadcasted_iota(jnp.int32, (3 * LANES, wide), 1)
    head = row & (LANES - 1)
    part = row >> 7
    live = head < n_heads
    e_q = jnp.where((col == head * GATE_COLS + part) & live, 1.0, 0.0).astype(BF16)
    e_k = jnp.where((col == head * GATE_COLS + 3 + part) & live, 1.0, 0.0).astype(BF16)
    lane = lax.broadcasted_iota(jnp.int32, (1, wide), 1)
    pos = lane & (GATE_COLS - 1)
    valid = (lane >> 3) < n_heads
    ones_q = jnp.where((pos >= 3) & (pos < 6) & valid, 1.0, 0.0)
    ones_k = jnp.where((pos < 3) & valid, 1.0, 0.0)
    qx = jnp.dot(parts, e_q, preferred_element_type=F32) + ones_q
    kx = ones_k - jnp.dot(parts, e_k, preferred_element_type=F32)
    for half in range(2):
        qx_ref[0, half] = qx[:, half * LANES:(half + 1) * LANES].astype(BF16)
        kx_ref[0, half] = kx[:, half * LANES:(half + 1) * LANES].astype(BF16)


def _forget_cumsum(f, b_forget, *, batch, seq, blk=512):
    nb = seq // blk
    n_heads = b_forget.shape[0]
    assert n_heads * GATE_COLS <= 2 * LANES
    b_r = jnp.pad(b_forget.astype(F32), (0, LANES - n_heads)).reshape(1, LANES)
    body = functools.partial(_forget_cumsum_body, blk=blk, n_heads=n_heads)
    xspec = pl.BlockSpec((1, 2, blk, LANES), lambda b, i: (b, 0, i, 0))
    xshape = jax.ShapeDtypeStruct((batch, 2, seq, LANES), BF16)
    return pl.pallas_call(
        body,
        grid=(batch, nb),
        in_specs=[
            pl.BlockSpec((blk, LANES), lambda b, i: (b * nb + i, 0)),
            pl.BlockSpec((1, LANES), lambda b, i: (0, 0)),
        ],
        out_specs=[xspec, xspec],
        out_shape=[xshape, xshape],
        scratch_shapes=[pltpu.VMEM((1, LANES), F32)],
        compiler_params=pltpu.CompilerParams(
            dimension_semantics=("parallel", "arbitrary"), vmem_limit_bytes=VMEM_LIMIT),
        name="forget_cumsum",
    )(f, b_r)


V_ROWS = F_HEAD_DIM + 16


def _fox_attn_body(q_ref, k_ref, v_ref, z_ref, qx_ref, kx_ref, o_ref,
                   kxm_scr, vt_scr, acc_scr, *, seq, tq, tk):
    h = pl.program_id(1)
    nq = seq // tq
    assert tq == tk and seq % tq == 0

    lane = lax.broadcasted_iota(jnp.int32, (seq, LANES), 1)
    mine = (lane >> 3) == (h % GATE_HEADS_PER_BLOCK)
    kxm_scr[...] = jnp.where(mine, kx_ref[0, 0].astype(F32), 0.0).astype(BF16)

    def vt_block(i, carry):
        rows = pl.ds(pl.multiple_of(i * LANES, LANES), LANES)
        vt_scr[0:F_HEAD_DIM, rows] = v_ref[0, rows, :].astype(F32).T.astype(BF16)
        return carry

    lax.fori_loop(0, seq // LANES, vt_block, 0)
    r16 = lax.broadcasted_iota(jnp.int32, (V_ROWS - F_HEAD_DIM, seq), 0)
    vt_scr[F_HEAD_DIM:V_ROWS, :] = jnp.where(r16 == 0, 1.0, 0.0).astype(BF16)

    kpos = lax.broadcasted_iota(jnp.int32, (tk, tq), 0)
    qpos = lax.broadcasted_iota(jnp.int32, (tk, tq), 1)
    causal = kpos <= qpos

    for qi in range(nq):
        qrows = pl.ds(qi * tq, tq)
        qa = jnp.concatenate([q_ref[0, qrows, :], qx_ref[0, 0, qrows, :]], axis=1)
        m = jnp.full((1, tq), NEG_INF, F32)
        acc_scr[...] = jnp.zeros(acc_scr.shape, F32)
        for kj in range(qi + 1):
            krows = pl.ds(kj * tk, tk)
            ka = jnp.concatenate([k_ref[0, krows, :], kxm_scr[krows, :]], axis=1)
            st = lax.dot_general(ka, qa, _NT, preferred_element_type=F32)
            if kj == qi:
                st = jnp.where(causal, st, NEG_INF)
            m_new = jnp.maximum(m, jnp.max(st, axis=0, keepdims=True))
            alpha = jnp.exp2(m - m_new)
            pt = jnp.exp2(st - m_new).astype(BF16)
            pv = jnp.dot(vt_scr[:, krows], pt, preferred_element_type=F32)
            acc_scr[...] = acc_scr[...] * alpha + pv
            m = m_new
        ot = acc_scr[0:F_HEAD_DIM, :] / acc_scr[F_HEAD_DIM:F_HEAD_DIM + 1, :]
        zv = z_ref[0, qrows, :].astype(F32)
        o_ref[qrows, :] = (ot.T * (zv * _sigmoid(zv))).astype(BF16)


def _fox_attention(p_hm, qx, kx, *, batch, seq, n_heads, tq=512, tk=512):
    t = batch * seq
    e_width = n_heads * F_HEAD_DIM
    body = functools.partial(_fox_attn_body, seq=seq, tq=tq, tk=tk)
    hm = lambda off: pl.BlockSpec((1, seq, F_HEAD_DIM), lambda b, h: (off + h, b, 0))
    xspec = pl.BlockSpec((1, 1, seq, LANES), lambda b, h: (b, h // GATE_HEADS_PER_BLOCK, 0, 0))
    return pl.pallas_call(
        body,
        grid=(batch, n_heads),
        in_specs=[hm(0), hm(n_heads), hm(2 * n_heads), hm(3 * n_heads), xspec, xspec],
        out_specs=pl.BlockSpec((seq, F_HEAD_DIM), lambda b, h: (b, h)),
        out_shape=jax.ShapeDtypeStruct((t, e_width), BF16),
        scratch_shapes=[
            pltpu.VMEM((seq, LANES), BF16),
            pltpu.VMEM((V_ROWS, seq), BF16),
            pltpu.VMEM((V_ROWS, tq), F32),
        ],
        compiler_params=pltpu.CompilerParams(
            dimension_semantics=("parallel", "arbitrary"), vmem_limit_bytes=VMEM_LIMIT),
        name="fox_attention",
    )(p_hm, p_hm, p_hm, p_hm, qx, kx)


def kernel(x, norm_w, m_w_in, m_conv_w, m_conv_b, m_dt_bias, m_A_log, m_D, m_norm_w, m_w_out,
           f_w_in, f_b_forget, f_w_out, final_norm_w):
    batch, seq, d_model = x.shape
    t = batch * seq
    e_width = m_w_out.shape[1]
    m_heads = m_dt_bias.shape[1]
    f_heads = f_b_forget.shape[1]
    conv_dim = m_conv_w.shape[2]
    assert norm_w.shape[0] == 2 and m_w_in.shape[0] == 1 and f_w_in.shape[0] == 1
    assert e_width == m_heads * M_HEAD_DIM == f_heads * F_HEAD_DIM

    xf = x.reshape(t, d_model).astype(F32)

    w_in = m_w_in[0]
    n_main = e_width + conv_dim
    w_main = w_in[:, :n_main].astype(BF16)
    w_dt = jnp.pad(w_in[:, n_main:], ((0, 0), (0, LANES - m_heads))).astype(BF16)
    zx, dt_raw, dtt_raw = _norm_matmul(xf, norm_w[0], w_main, w_dt, head_major=False,
                                       with_transposed=True)
    yg = _ssd(zx, dt_raw, dtt_raw, m_conv_w[0], m_conv_b[0], m_dt_bias[0], m_A_log[0], m_D[0],
              m_norm_w[0], batch=batch, seq=seq, e_width=e_width)
    x1 = _out_proj(yg, m_w_out[0].astype(BF16), xf, None)

    fw_in = f_w_in[0]
    fw_main = fw_in[:, :4 * e_width].astype(BF16)
    fw_f = jnp.pad(fw_in[:, 4 * e_width:], ((0, 0), (0, LANES - f_heads))).astype(BF16)
    p_hm, f_raw = _norm_matmul(x1, norm_w[1], fw_main, fw_f, head_major=True, with_transposed=False,
                               lead_cols=e_width, lead_scale=LOG2E / math.sqrt(F_HEAD_DIM))
    qx, kx = _forget_cumsum(f_raw, f_b_forget[0], batch=batch, seq=seq)
    og = _fox_attention(p_hm, qx, kx, batch=batch, seq=seq, n_heads=f_heads)
    out = _out_proj(og, f_w_out[0].astype(BF16), x1, final_norm_w)
    return out.reshape(batch, seq, d_model)
```

```python
import functools
import math

import jax
import jax.numpy as jnp
from jax import lax
from jax.experimental import pallas as pl
from jax.experimental.pallas import tpu as pltpu

F32 = jnp.float32
BF16 = jnp.bfloat16
EPS = 1e-6
LANES = 128
LOG2E = 1.4426950408889634
NEG_INF = float("-inf")
HIGHEST = lax.Precision.HIGHEST
VMEM_LIMIT = 56 * 1024 * 1024

M_HEAD_DIM = 64
M_GROUPS = 8
M_STATE = 128
M_CONV = 4
SSD_CHUNK = 128
CONV_TAIL = 16
F_HEAD_DIM = 128

_NT = (((1,), (1,)), ((), ()))


def _sigmoid(v):
    return 1.0 / (1.0 + jnp.exp(-v))


def _softplus(v):
    return jnp.maximum(v, 0.0) + jnp.log(1.0 + jnp.exp(-jnp.abs(v)))


def _log_sigmoid(v):
    return jnp.minimum(v, 0.0) - jnp.log1p(jnp.exp(-jnp.abs(v)))


def _norm_matmul_body(x_ref, nw_ref, w_ref, ws_ref, o_ref, os_ref, *rest,
                      tm, tn, head_major, row_chunk, with_transposed, lead_tiles, lead_scale):
    if with_transposed:
        ost_ref, h_scr = rest
    else:
        (h_scr,) = rest
    j = pl.program_id(1)

    @pl.when(j == 0)
    def _():
        nw = nw_ref[...]

        def norm_rows(r, carry):
            rows = pl.ds(pl.multiple_of(r * row_chunk, row_chunk), row_chunk)
            xv = x_ref[rows, :]
            ms = jnp.mean(xv * xv, axis=-1, keepdims=True)
            h_scr[rows, :] = ((xv * lax.rsqrt(ms + EPS)) * nw).astype(BF16)
            return carry

        lax.fori_loop(0, tm // row_chunk, norm_rows, 0)
        small = jnp.dot(h_scr[...], ws_ref[...], preferred_element_type=F32)
        os_ref[...] = small
        if with_transposed:
            ost_ref[...] = small.T

    res = jnp.dot(h_scr[...], w_ref[...], preferred_element_type=F32)
    if lead_tiles:
        res = res * jnp.where(j < lead_tiles, jnp.float32(lead_scale), jnp.float32(1.0))
    res = res.astype(o_ref.dtype)
    if head_major:
        for c in range(tn // LANES):
            o_ref[c] = res[:, c * LANES:(c + 1) * LANES]
    else:
        o_ref[...] = res


def _norm_matmul(x2d, nw, w_all, n, w_small, *, head_major, with_transposed, lead_cols=0, lead_scale=1.0,
                 tm=1024, tn=1024):
    t, d = x2d.shape
    assert t % tm == 0 and n % tn == 0 and w_small.shape == (d, LANES) and lead_cols % tn == 0
    assert w_all.shape[0] == d and w_all.shape[1] >= n
    grid = (t // tm, n // tn)
    if head_major:
        main_shape = jax.ShapeDtypeStruct((n // LANES, t, LANES), BF16)
        main_spec = pl.BlockSpec((tn // LANES, tm, LANES), lambda i, j: (j, i, 0))
    else:
        main_shape = jax.ShapeDtypeStruct((t, n), BF16)
        main_spec = pl.BlockSpec((tm, tn), lambda i, j: (i, j))
    out_specs = [main_spec, pl.BlockSpec((tm, LANES), lambda i, j: (i, 0))]
    out_shape = [main_shape, jax.ShapeDtypeStruct((t, LANES), F32)]
    if with_transposed:
        out_specs.append(pl.BlockSpec((LANES, tm), lambda i, j: (0, i)))
        out_shape.append(jax.ShapeDtypeStruct((LANES, t), F32))
    body = functools.partial(_norm_matmul_body, tm=tm, tn=tn, head_major=head_major, row_chunk=64,
                             with_transposed=with_transposed, lead_tiles=lead_cols // tn,
                             lead_scale=lead_scale)
    return pl.pallas_call(
        body,
        grid=grid,
        in_specs=[
            pl.BlockSpec((tm, d), lambda i, j: (i, 0)),
            pl.BlockSpec((1, d), lambda i, j: (0, 0)),
            pl.BlockSpec((d, tn), lambda i, j: (0, j)),
            pl.BlockSpec((d, LANES), lambda i, j: (0, 0)),
        ],
        out_specs=out_specs,
        out_shape=out_shape,
        scratch_shapes=[pltpu.VMEM((tm, d), BF16)],
        compiler_params=pltpu.CompilerParams(
            dimension_semantics=("parallel", "arbitrary"), vmem_limit_bytes=VMEM_LIMIT),
        name="norm_in_proj_hm" if head_major else "norm_in_proj",
    )(x2d, nw.reshape(1, d), w_all, w_small)


def _ssd_body(z_ref, x_ref, b_ref, c_ref, dt_ref, dtt_ref, cw_ref, cbias_ref,
              dtb_r_ref, dtb_c_ref, alog_r_ref, alog_c_ref, dskip_ref, nw_ref,
              o_ref,
              ext, xc, st, ysc, acs_s, acst_s, wt_s, eacs_s, cd_s,
              *, e_width, gn):
    q = SSD_CHUNK
    conv_dim = e_width + 2 * gn
    cidx = pl.program_id(1)

    @pl.when(cidx == 0)
    def _():
        ext[0:CONV_TAIL, :] = jnp.zeros((CONV_TAIL, conv_dim), BF16)
        st[...] = jnp.zeros(st.shape, F32)

    ext[pl.ds(CONV_TAIL, q), 0:e_width] = x_ref[...]
    ext[pl.ds(CONV_TAIL, q), e_width:e_width + gn] = b_ref[...]
    ext[pl.ds(CONV_TAIL, q), e_width + gn:conv_dim] = c_ref[...]

    n_sh = M_CONV - 1
    prow = lax.broadcasted_iota(jnp.int32, (n_sh * q, q + CONV_TAIL), 0)
    pcol = lax.broadcasted_iota(jnp.int32, (n_sh * q, q + CONV_TAIL), 1)
    tap = prow >> 7
    shift_sel = jnp.where(pcol == (prow & (q - 1)) + tap + (CONV_TAIL - n_sh), 1.0, 0.0).astype(BF16)
    lane_chunk = 512
    for i in range(conv_dim // lane_chunk):
        lanes = slice(i * lane_chunk, (i + 1) * lane_chunk)
        shifted = jnp.dot(shift_sel, ext[:, lanes], preferred_element_type=F32)
        acc = cbias_ref[:, lanes] + cw_ref[n_sh:M_CONV, lanes] * ext[pl.ds(CONV_TAIL, q), lanes].astype(F32)
        for k in range(n_sh):
            acc = acc + cw_ref[k:k + 1, lanes] * shifted[k * q:(k + 1) * q]
        xc[:, lanes] = acc * _sigmoid(acc)
    ext[0:CONV_TAIL, :] = ext[pl.ds(q, CONV_TAIL), :]

    li = lax.broadcasted_iota(jnp.int32, (q, q), 0)
    lj = lax.broadcasted_iota(jnp.int32, (q, q), 1)
    causal = lj <= li
    tril = causal.astype(F32)
    triu = (li <= lj).astype(F32)

    dt = _softplus(dt_ref[...] + dtb_r_ref[...])
    dtt = _softplus(dtt_ref[...] + dtb_c_ref[...])
    a_r = -jnp.exp(alog_r_ref[...])
    a_c = -jnp.exp(alog_c_ref[...])
    acs = jnp.dot(tril, dt * a_r, preferred_element_type=F32, precision=HIGHEST)
    acst = jnp.dot(dtt * a_c, triu, preferred_element_type=F32, precision=HIGHEST)
    acs_s[...] = acs
    acst_s[...] = acst - jnp.log(dtt)
    wt_s[...] = jnp.exp(acst[:, q - 1:q] - acst) * dtt
    eacs_s[...] = jnp.exp(acs)
    cd_s[...] = jnp.exp(acs[q - 1:q, :])

    lane = lax.broadcasted_iota(jnp.int32, (q, LANES), 1)
    lo_half = lane < M_HEAD_DIM
    lane_row = lax.broadcasted_iota(jnp.int32, (1, LANES), 1)
    lo_half_row = lane_row < M_HEAD_DIM
    hpg = e_width // M_HEAD_DIM // M_GROUPS
    gw = hpg * M_HEAD_DIM

    for g in range(M_GROUPS):
        bg = xc[:, e_width + g * M_STATE:e_width + (g + 1) * M_STATE]
        cg = xc[:, e_width + gn + g * M_STATE:e_width + gn + (g + 1) * M_STATE]
        bgb = bg.astype(BF16)
        cgb = cg.astype(BF16)
        cb = lax.dot_general(cgb, bgb, _NT, preferred_element_type=F32)
        bgt = bg.T
        cs = jnp.dot(cgb, st[:, g * gw:(g + 1) * gw].astype(BF16),
                     preferred_element_type=F32)
        for pr in range(hpg // 2):
            h0 = g * hpg + 2 * pr
            c0 = g * gw + pr * LANES
            xs_f = xc[:, c0:c0 + LANES]
            xs_b = xs_f.astype(BF16)
            zero = jnp.zeros_like(xs_b)
            rhs = jnp.concatenate([jnp.where(lo_half, xs_b, zero),
                                   jnp.where(lo_half, zero, xs_b)], axis=0)
            m_parts = []
            w_parts = []
            for hh in (h0, h0 + 1):
                seg = acs_s[:, hh:hh + 1] - acst_s[hh:hh + 1, :]
                lm = jnp.exp(jnp.where(causal, seg, NEG_INF))
                m_parts.append((cb * lm).astype(BF16))
                w_parts.append((bgt * wt_s[hh:hh + 1, :]).astype(BF16))
            lhs = jnp.concatenate([jnp.concatenate(m_parts, axis=1),
                                   jnp.concatenate(w_parts, axis=1)], axis=0)
            res = jnp.dot(lhs, rhs, preferred_element_type=F32)
            y_diag = res[0:q]
            s_new = res[q:2 * q]
            e_pair = jnp.where(lo_half, eacs_s[:, h0:h0 + 1], eacs_s[:, h0 + 1:h0 + 2])
            y_pair = (y_diag + e_pair * cs[:, pr * LANES:(pr + 1) * LANES]
                      + xs_f * dskip_ref[:, c0:c0 + LANES])
            cd_pair = jnp.where(lo_half_row, cd_s[:, h0:h0 + 1], cd_s[:, h0 + 1:h0 + 2])
            st[:, c0:c0 + LANES] = st[:, c0:c0 + LANES] * cd_pair + s_new
            ysc[:, c0:c0 + LANES] = y_pair
        zg = z_ref[:, g * gw:(g + 1) * gw].astype(F32)
        gt = ysc[:, g * gw:(g + 1) * gw] * (zg * _sigmoid(zg))
        ms = jnp.mean(gt * gt, axis=-1, keepdims=True)
        o_ref[:, g * gw:(g + 1) * gw] = (
            (gt * lax.rsqrt(ms + EPS)) * nw_ref[:, g * gw:(g + 1) * gw]).astype(BF16)


def _ssd(zx, dt_raw, dtt_raw, conv_w, conv_b, dt_bias, a_log, d_skip, norm_w, *, batch, seq, e_width):
    t = batch * seq
    gn = M_GROUPS * M_STATE
    conv_dim = e_width + 2 * gn
    q = SSD_CHUNK
    nc = seq // q
    n_heads = e_width // M_HEAD_DIM
    assert zx.shape == (t, e_width + conv_dim) and seq % q == 0
    assert e_width % gn == 0 and gn % LANES == 0

    def pad_row(v):
        return jnp.pad(v.astype(F32), (0, LANES - n_heads)).reshape(1, LANES)

    dtb_r = pad_row(dt_bias)
    alog_r = pad_row(a_log)
    dskip = jnp.repeat(d_skip.astype(F32), M_HEAD_DIM).reshape(1, e_width)
    rowblk = lambda b, c: b * nc + c
    const = lambda b, c: (0, 0)
    body = functools.partial(_ssd_body, e_width=e_width, gn=gn)
    return pl.pallas_call(
        body,
        grid=(batch, nc),
        in_specs=[
            pl.BlockSpec((q, e_width), lambda b, c: (rowblk(b, c), 0)),
            pl.BlockSpec((q, e_width), lambda b, c: (rowblk(b, c), 1)),
            pl.BlockSpec((q, gn), lambda b, c: (rowblk(b, c), 2 * e_width // gn)),
            pl.BlockSpec((q, gn), lambda b, c: (rowblk(b, c), 2 * e_width // gn + 1)),
            pl.BlockSpec((q, LANES), lambda b, c: (rowblk(b, c), 0)),
            pl.BlockSpec((LANES, q), lambda b, c: (0, rowblk(b, c))),
            pl.BlockSpec((M_CONV, conv_dim), const),
            pl.BlockSpec((1, conv_dim), const),
            pl.BlockSpec((1, LANES), const),
            pl.BlockSpec((LANES, 1), const),
            pl.BlockSpec((1, LANES), const),
            pl.BlockSpec((LANES, 1), const),
            pl.BlockSpec((1, e_width), const),
            pl.BlockSpec((1, e_width), const),
        ],
        out_specs=pl.BlockSpec((q, e_width), lambda b, c: (rowblk(b, c), 0)),
        out_shape=jax.ShapeDtypeStruct((t, e_width), BF16),
        scratch_shapes=[
            pltpu.VMEM((q + CONV_TAIL, conv_dim), BF16),
            pltpu.VMEM((q, conv_dim), F32),
            pltpu.VMEM((M_STATE, e_width), F32),
            pltpu.VMEM((q, e_width), F32),
            pltpu.VMEM((q, LANES), F32),
            pltpu.VMEM((LANES, q), F32),
            pltpu.VMEM((LANES, q), F32),
            pltpu.VMEM((q, LANES), F32),
            pltpu.VMEM((1, LANES), F32),
        ],
        compiler_params=pltpu.CompilerParams(
            dimension_semantics=("parallel", "arbitrary"), vmem_limit_bytes=VMEM_LIMIT),
        name="ssd_mixer",
    )(zx, zx, zx, zx, dt_raw, dtt_raw, conv_w.astype(F32), conv_b.astype(F32).reshape(1, conv_dim),
      dtb_r, dtb_r.reshape(LANES, 1), alog_r, alog_r.reshape(LANES, 1), dskip,
      norm_w.astype(F32).reshape(1, e_width))


def _out_proj_body(g_ref, w_ref, x_ref, fw_ref, o_ref, *, tn, final_norm):
    j = pl.program_id(1)
    cols = pl.ds(pl.multiple_of(j * tn, tn), tn)
    o_ref[:, cols] = x_ref[:, cols] + jnp.dot(g_ref[...], w_ref[...], preferred_element_type=F32)

    if final_norm:
        @pl.when(j == pl.num_programs(1) - 1)
        def _():
            r = o_ref[...]
            ms = jnp.mean(r * r, axis=-1, keepdims=True)
            o_ref[...] = (r * lax.rsqrt(ms + EPS)) * fw_ref[...]


def _out_proj(g, w, x2d, final_w, *, tm=512, tn=512):
    t, kdim = g.shape
    d = w.shape[1]
    assert t % tm == 0 and d % tn == 0
    final_norm = final_w is not None
    fw = (final_w if final_norm else jnp.ones((d,), F32)).astype(F32).reshape(1, d)
    body = functools.partial(_out_proj_body, tn=tn, final_norm=final_norm)
    return pl.pallas_call(
        body,
        grid=(t // tm, d // tn),
        in_specs=[
            pl.BlockSpec((tm, kdim), lambda i, j: (i, 0)),
            pl.BlockSpec((kdim, tn), lambda i, j: (0, j)),
            pl.BlockSpec((tm, d), lambda i, j: (i, 0)),
            pl.BlockSpec((1, d), lambda i, j: (0, 0)),
        ],
        out_specs=pl.BlockSpec((tm, d), lambda i, j: (i, 0)),
        out_shape=jax.ShapeDtypeStruct((t, d), F32),
        compiler_params=pltpu.CompilerParams(
            dimension_semantics=("parallel", "arbitrary"), vmem_limit_bytes=VMEM_LIMIT),
        name="out_proj_final" if final_norm else "out_proj",
    )(g, w, x2d, fw)


GATE_COLS = 8
GATE_HEADS_PER_BLOCK = LANES // GATE_COLS


def _forget_cumsum_body(f_ref, b_r_ref, qx_ref, kx_ref, carry_r, *, blk, n_heads):
    i = pl.program_id(1)

    @pl.when(i == 0)
    def _():
        carry_r[...] = jnp.zeros(carry_r.shape, F32)

    li = lax.broadcasted_iota(jnp.int32, (blk, blk), 0)
    lj = lax.broadcasted_iota(jnp.int32, (blk, blk), 1)
    tril = (lj <= li).astype(F32)
    lf = _log_sigmoid(f_ref[...] + b_r_ref[...])
    cum = jnp.dot(tril, lf, preferred_element_type=F32, precision=HIGHEST) + carry_r[...]
    carry_r[...] = cum[blk - 1:blk, :]
    c2 = cum * LOG2E
    hi = c2.astype(BF16)
    r1 = c2 - hi.astype(F32)
    mid = r1.astype(BF16)
    lo = (r1 - mid.astype(F32)).astype(BF16)
    parts = jnp.concatenate([hi, mid, lo], axis=1)
    wide = 2 * LANES
    row = lax.broadcasted_iota(jnp.int32, (3 * LANES, wide), 0)
    col = lax.broadcasted_iota(jnp.int32, (3 * LANES, wide), 1)
    head = row & (LANES - 1)
    part = row >> 7
    live = head < n_heads
    e_q = jnp.where((col == head * GATE_COLS + part) & live, 1.0, 0.0).astype(BF16)
    e_k = jnp.where((col == head * GATE_COLS + 3 + part) & live, 1.0, 0.0).astype(BF16)
    lane = lax.broadcasted_iota(jnp.int32, (1, wide), 1)
    pos = lane & (GATE_COLS - 1)
    valid = (lane >> 3) < n_heads
    ones_q = jnp.where((pos >= 3) & (pos < 6) & valid, 1.0, 0.0)
    ones_k = jnp.where((pos < 3) & valid, 1.0, 0.0)
    qx = jnp.dot(parts, e_q, preferred_element_type=F32) + ones_q
    kx = ones_k - jnp.dot(parts, e_k, preferred_element_type=F32)
    for half in range(2):
        qx_ref[0, half] = qx[:, half * LANES:(half + 1) * LANES].astype(BF16)
        kx_ref[0, half] = kx[:, half * LANES:(half + 1) * LANES].astype(BF16)


def _forget_cumsum(f, b_forget, *, batch, seq, blk=512):
    nb = seq // blk
    n_heads = b_forget.shape[0]
    assert n_heads * GATE_COLS <= 2 * LANES
    b_r = jnp.pad(b_forget.astype(F32), (0, LANES - n_heads)).reshape(1, LANES)
    body = functools.partial(_forget_cumsum_body, blk=blk, n_heads=n_heads)
    xspec = pl.BlockSpec((1, 2, blk, LANES), lambda b, i: (b, 0, i, 0))
    xshape = jax.ShapeDtypeStruct((batch, 2, seq, LANES), BF16)
    return pl.pallas_call(
        body,
        grid=(batch, nb),
        in_specs=[
            pl.BlockSpec((blk, LANES), lambda b, i: (b * nb + i, 0)),
            pl.BlockSpec((1, LANES), lambda b, i: (0, 0)),
        ],
        out_specs=[xspec, xspec],
        out_shape=[xshape, xshape],
        scratch_shapes=[pltpu.VMEM((1, LANES), F32)],
        compiler_params=pltpu.CompilerParams(
            dimension_semantics=("parallel", "arbitrary"), vmem_limit_bytes=VMEM_LIMIT),
        name="forget_cumsum",
    )(f, b_r)


V_ROWS = F_HEAD_DIM + 16


def _fox_attn_body(q_ref, k_ref, v_ref, z_ref, qx_ref, kx_ref, o_ref,
                   kxm_scr, vt_scr, acc_scr, *, seq, tq, tk, hpb):
    pair = pl.program_id(1)
    nq = seq // tq
    assert tq == tk and seq % tq == 0

    lane = lax.broadcasted_iota(jnp.int32, (seq, LANES), 1)
    kxf = kx_ref[0, 0].astype(F32)
    r16 = lax.broadcasted_iota(jnp.int32, (V_ROWS - F_HEAD_DIM, seq), 0)
    ones_rows = jnp.where(r16 == 0, 1.0, 0.0).astype(BF16)
    for hh in range(hpb):
        mine = (lane >> 3) == ((pair * hpb + hh) % GATE_HEADS_PER_BLOCK)
        kxm_scr[hh] = jnp.where(mine, kxf, 0.0).astype(BF16)
        vt_scr[hh, F_HEAD_DIM:V_ROWS, :] = ones_rows

    kpos = lax.broadcasted_iota(jnp.int32, (tk, tq), 0)
    qpos = lax.broadcasted_iota(jnp.int32, (tk, tq), 1)
    causal = kpos <= qpos

    for qi in range(nq):
        qrows = pl.ds(qi * tq, tq)
        qa = []
        for hh in range(hpb):
            for c in range(tq // LANES):
                rows = pl.ds(qi * tq + c * LANES, LANES)
                vt_scr[hh, 0:F_HEAD_DIM, rows] = v_ref[hh, rows, :].astype(F32).T.astype(BF16)
            qcat = jnp.concatenate([q_ref[hh, qrows, :], qx_ref[0, 0, qrows, :]], axis=1)
            qa.append(qcat.astype(F32).T.astype(BF16))
        m = [None] * hpb
        for kj in range(qi + 1):
            krows = pl.ds(kj * tk, tk)
            for hh in range(hpb):
                ka = jnp.concatenate([k_ref[hh, krows, :], kxm_scr[hh, krows, :]], axis=1)
                st = jnp.dot(ka, qa[hh], preferred_element_type=F32)
                if kj == qi:
                    st = jnp.where(causal, st, NEG_INF)
                mb = jnp.max(st, axis=0, keepdims=True)
                m_new = mb if kj == 0 else jnp.maximum(m[hh], mb)
                pt = jnp.exp2(st - m_new).astype(BF16)
                pv = jnp.dot(vt_scr[hh, :, krows], pt, preferred_element_type=F32)
                if kj == 0:
                    acc_scr[hh] = pv
                else:
                    acc_scr[hh] = acc_scr[hh] * jnp.exp2(m[hh] - m_new) + pv
                m[hh] = m_new
        for hh in range(hpb):
            ot = acc_scr[hh, 0:F_HEAD_DIM, :] / acc_scr[hh, F_HEAD_DIM:F_HEAD_DIM + 1, :]
            zv = z_ref[hh, qrows, :].astype(F32)
            o_ref[qrows, hh * F_HEAD_DIM:(hh + 1) * F_HEAD_DIM] = (
                ot.T * (zv * _sigmoid(zv))).astype(BF16)


def _fox_attention(p_hm, qx, kx, *, batch, seq, n_heads, tq=512, tk=512, hpb=2):
    t = batch * seq
    e_width = n_heads * F_HEAD_DIM
    assert n_heads % hpb == 0 and GATE_HEADS_PER_BLOCK % hpb == 0
    body = functools.partial(_fox_attn_body, seq=seq, tq=tq, tk=tk, hpb=hpb)
    hm = lambda off: pl.BlockSpec((hpb, seq, F_HEAD_DIM), lambda b, p: (off // hpb + p, b, 0))
    xspec = pl.BlockSpec((1, 1, seq, LANES), lambda b, p: (b, (p * hpb) // GATE_HEADS_PER_BLOCK, 0, 0))
    return pl.pallas_call(
        body,
        grid=(batch, n_heads // hpb),
        in_specs=[hm(0), hm(n_heads), hm(2 * n_heads), hm(3 * n_heads), xspec, xspec],
        out_specs=pl.BlockSpec((seq, hpb * F_HEAD_DIM), lambda b, p: (b, p)),
        out_shape=jax.ShapeDtypeStruct((t, e_width), BF16),
        scratch_shapes=[
            pltpu.VMEM((hpb, seq, LANES), BF16),
            pltpu.VMEM((hpb, V_ROWS, seq), BF16),
            pltpu.VMEM((hpb, V_ROWS, tq), F32),
        ],
        compiler_params=pltpu.CompilerParams(
            dimension_semantics=("parallel", "arbitrary"), vmem_limit_bytes=VMEM_LIMIT),
        name="fox_attention",
    )(p_hm, p_hm, p_hm, p_hm, qx, kx)


def kernel(x, norm_w, m_w_in, m_conv_w, m_conv_b, m_dt_bias, m_A_log, m_D, m_norm_w, m_w_out,
           f_w_in, f_b_forget, f_w_out, final_norm_w):
    batch, seq, d_model = x.shape
    t = batch * seq
    e_width = m_w_out.shape[1]
    m_heads = m_dt_bias.shape[1]
    f_heads = f_b_forget.shape[1]
    conv_dim = m_conv_w.shape[2]
    assert norm_w.shape[0] == 2 and m_w_in.shape[0] == 1 and f_w_in.shape[0] == 1
    assert e_width == m_heads * M_HEAD_DIM == f_heads * F_HEAD_DIM

    xf = x.reshape(t, d_model).astype(F32)

    w_in = m_w_in[0].astype(BF16)
    n_main = e_width + conv_dim
    w_dt = jnp.pad(w_in[:, n_main:], ((0, 0), (0, LANES - m_heads)))
    zx, dt_raw, dtt_raw = _norm_matmul(xf, norm_w[0], w_in, n_main, w_dt, head_major=False,
                                       with_transposed=True)
    yg = _ssd(zx, dt_raw, dtt_raw, m_conv_w[0], m_conv_b[0], m_dt_bias[0], m_A_log[0], m_D[0],
              m_norm_w[0], batch=batch, seq=seq, e_width=e_width)
    x1 = _out_proj(yg, m_w_out[0].astype(BF16), xf, None)

    fw_in = f_w_in[0].astype(BF16)
    fw_f = jnp.pad(fw_in[:, 4 * e_width:], ((0, 0), (0, LANES - f_heads)))
    p_hm, f_raw = _norm_matmul(x1, norm_w[1], fw_in, 4 * e_width, fw_f, head_major=True,
                               with_transposed=False,
                               lead_cols=e_width, lead_scale=LOG2E / math.sqrt(F_HEAD_DIM))
    qx, kx = _forget_cumsum(f_raw, f_b_forget[0], batch=batch, seq=seq)
    og = _fox_attention(p_hm, qx, kx, batch=batch, seq=seq, n_heads=f_heads)
    out = _out_proj(og, f_w_out[0].astype(BF16), x1, final_norm_w)
    return out.reshape(batch, seq, d_model)
```

```python
import functools
import math

import jax
import jax.numpy as jnp
from jax import lax
from jax.experimental import pallas as pl
from jax.experimental.pallas import tpu as pltpu

F32 = jnp.float32
BF16 = jnp.bfloat16
EPS = 1e-6
LANES = 128
LOG2E = 1.4426950408889634
NEG_INF = float("-inf")
HIGHEST = lax.Precision.HIGHEST
VMEM_LIMIT = 56 * 1024 * 1024

M_HEAD_DIM = 64
M_GROUPS = 8
M_STATE = 128
M_CONV = 4
SSD_CHUNK = 128
CONV_TAIL = 16
F_HEAD_DIM = 128

_NT = (((1,), (1,)), ((), ()))


def _sigmoid(v):
    return 1.0 / (1.0 + jnp.exp(-v))


def _softplus(v):
    return jnp.maximum(v, 0.0) + jnp.log(1.0 + jnp.exp(-jnp.abs(v)))


def _log_sigmoid(v):
    return jnp.minimum(v, 0.0) - jnp.log1p(jnp.exp(-jnp.abs(v)))


def _norm_matmul_body(x_ref, nw_ref, wt_ref, wst_ref, o_ref, os_ref, *rest,
                      tm, tn, head_major, row_chunk, with_transposed, lead_tiles, lead_scale):
    if with_transposed:
        ost_ref, h_scr = rest
    else:
        (h_scr,) = rest
    j = pl.program_id(1)

    @pl.when(j == 0)
    def _():
        nw = nw_ref[...]

        def norm_rows(r, carry):
            rows = pl.ds(pl.multiple_of(r * row_chunk, row_chunk), row_chunk)
            xv = x_ref[rows, :]
            ms = jnp.mean(xv * xv, axis=-1, keepdims=True)
            h_scr[rows, :] = ((xv * lax.rsqrt(ms + EPS)) * nw).astype(BF16)
            return carry

        lax.fori_loop(0, tm // row_chunk, norm_rows, 0)
        small = lax.dot_general(h_scr[...], wst_ref[...], _NT, preferred_element_type=F32)
        os_ref[...] = small
        if with_transposed:
            ost_ref[...] = small.T

    res = lax.dot_general(h_scr[...], wt_ref[...].astype(BF16), _NT, preferred_element_type=F32)
    if lead_tiles:
        res = res * jnp.where(j < lead_tiles, jnp.float32(lead_scale), jnp.float32(1.0))
    res = res.astype(o_ref.dtype)
    if head_major:
        for c in range(tn // LANES):
            o_ref[c] = res[:, c * LANES:(c + 1) * LANES]
    else:
        o_ref[...] = res


def _norm_matmul(x2d, nw, wt_all, n, wt_small, *, head_major, with_transposed, lead_cols=0,
                 lead_scale=1.0, tm=1024, tn=1024):
    t, d = x2d.shape
    assert t % tm == 0 and n % tn == 0 and wt_small.shape == (LANES, d) and lead_cols % tn == 0
    assert wt_all.shape[1] == d and wt_all.shape[0] >= n
    grid = (t // tm, n // tn)
    if head_major:
        main_shape = jax.ShapeDtypeStruct((n // LANES, t, LANES), BF16)
        main_spec = pl.BlockSpec((tn // LANES, tm, LANES), lambda i, j: (j, i, 0))
    else:
        main_shape = jax.ShapeDtypeStruct((t, n), BF16)
        main_spec = pl.BlockSpec((tm, tn), lambda i, j: (i, j))
    out_specs = [main_spec, pl.BlockSpec((tm, LANES), lambda i, j: (i, 0))]
    out_shape = [main_shape, jax.ShapeDtypeStruct((t, LANES), F32)]
    if with_transposed:
        out_specs.append(pl.BlockSpec((LANES, tm), lambda i, j: (0, i)))
        out_shape.append(jax.ShapeDtypeStruct((LANES, t), F32))
    body = functools.partial(_norm_matmul_body, tm=tm, tn=tn, head_major=head_major, row_chunk=64,
                             with_transposed=with_transposed, lead_tiles=lead_cols // tn,
                             lead_scale=lead_scale)
    return pl.pallas_call(
        body,
        grid=grid,
        in_specs=[
            pl.BlockSpec((tm, d), lambda i, j: (i, 0)),
            pl.BlockSpec((1, d), lambda i, j: (0, 0)),
            pl.BlockSpec((tn, d), lambda i, j: (j, 0)),
            pl.BlockSpec((LANES, d), lambda i, j: (0, 0)),
        ],
        out_specs=out_specs,
        out_shape=out_shape,
        scratch_shapes=[pltpu.VMEM((tm, d), BF16)],
        compiler_params=pltpu.CompilerParams(
            dimension_semantics=("parallel", "arbitrary"), vmem_limit_bytes=VMEM_LIMIT),
        name="norm_in_proj_hm" if head_major else "norm_in_proj",
    )(x2d, nw.reshape(1, d), wt_all, wt_small)


def _ssd_body(z_ref, x_ref, b_ref, c_ref, dt_ref, dtt_ref, cw_ref, cbias_ref,
              dtb_r_ref, dtb_c_ref, alog_r_ref, alog_c_ref, dskip_ref, nw_ref,
              o_ref,
              ext, xc, st, ysc, acs_s, acst_s, wt_s, eacs_s, cd_s,
              *, e_width, gn):
    q = SSD_CHUNK
    conv_dim = e_width + 2 * gn
    cidx = pl.program_id(1)

    @pl.when(cidx == 0)
    def _():
        ext[0:CONV_TAIL, :] = jnp.zeros((CONV_TAIL, conv_dim), BF16)
        st[...] = jnp.zeros(st.shape, F32)

    ext[pl.ds(CONV_TAIL, q), 0:e_width] = x_ref[...]
    ext[pl.ds(CONV_TAIL, q), e_width:e_width + gn] = b_ref[...]
    ext[pl.ds(CONV_TAIL, q), e_width + gn:conv_dim] = c_ref[...]

    n_sh = M_CONV - 1
    prow = lax.broadcasted_iota(jnp.int32, (n_sh * q, q + CONV_TAIL), 0)
    pcol = lax.broadcasted_iota(jnp.int32, (n_sh * q, q + CONV_TAIL), 1)
    tap = prow >> 7
    shift_sel = jnp.where(pcol == (prow & (q - 1)) + tap + (CONV_TAIL - n_sh), 1.0, 0.0).astype(BF16)
    lane_chunk = 512
    for i in range(conv_dim // lane_chunk):
        lanes = slice(i * lane_chunk, (i + 1) * lane_chunk)
        shifted = jnp.dot(shift_sel, ext[:, lanes], preferred_element_type=F32)
        acc = cbias_ref[:, lanes] + cw_ref[n_sh:M_CONV, lanes] * ext[pl.ds(CONV_TAIL, q), lanes].astype(F32)
        for k in range(n_sh):
            acc = acc + cw_ref[k:k + 1, lanes] * shifted[k * q:(k + 1) * q]
        xc[:, lanes] = acc * _sigmoid(acc)
    ext[0:CONV_TAIL, :] = ext[pl.ds(q, CONV_TAIL), :]

    li = lax.broadcasted_iota(jnp.int32, (q, q), 0)
    lj = lax.broadcasted_iota(jnp.int32, (q, q), 1)
    causal = lj <= li
    tril = causal.astype(F32)
    triu = (li <= lj).astype(F32)

    dt = _softplus(dt_ref[...] + dtb_r_ref[...])
    dtt = _softplus(dtt_ref[...] + dtb_c_ref[...])
    a_r = -jnp.exp(alog_r_ref[...])
    a_c = -jnp.exp(alog_c_ref[...])
    acs = jnp.dot(tril, dt * a_r, preferred_element_type=F32, precision=HIGHEST)
    acst = jnp.dot(dtt * a_c, triu, preferred_element_type=F32, precision=HIGHEST)
    acs_s[...] = acs
    acst_s[...] = acst - jnp.log(dtt)
    wt_s[...] = jnp.exp(acst[:, q - 1:q] - acst) * dtt
    eacs_s[...] = jnp.exp(acs)
    cd_s[...] = jnp.exp(acs[q - 1:q, :])

    lane = lax.broadcasted_iota(jnp.int32, (q, LANES), 1)
    lo_half = lane < M_HEAD_DIM
    lane_row = lax.broadcasted_iota(jnp.int32, (1, LANES), 1)
    lo_half_row = lane_row < M_HEAD_DIM
    hpg = e_width // M_HEAD_DIM // M_GROUPS
    gw = hpg * M_HEAD_DIM

    for g in range(M_GROUPS):
        bg = xc[:, e_width + g * M_STATE:e_width + (g + 1) * M_STATE]
        cg = xc[:, e_width + gn + g * M_STATE:e_width + gn + (g + 1) * M_STATE]
        bgb = bg.astype(BF16)
        cgb = cg.astype(BF16)
        cb = lax.dot_general(cgb, bgb, _NT, preferred_element_type=F32)
        bgt = bg.T
        cs = jnp.dot(cgb, st[:, g * gw:(g + 1) * gw].astype(BF16),
                     preferred_element_type=F32)
        for pr in range(hpg // 2):
            h0 = g * hpg + 2 * pr
            c0 = g * gw + pr * LANES
            xs_f = xc[:, c0:c0 + LANES]
            xs_b = xs_f.astype(BF16)
            zero = jnp.zeros_like(xs_b)
            rhs = jnp.concatenate([jnp.where(lo_half, xs_b, zero),
                                   jnp.where(lo_half, zero, xs_b)], axis=0)
            m_parts = []
            w_parts = []
            for hh in (h0, h0 + 1):
                seg = acs_s[:, hh:hh + 1] - acst_s[hh:hh + 1, :]
                lm = jnp.exp(jnp.where(causal, seg, NEG_INF))
                m_parts.append((cb * lm).astype(BF16))
                w_parts.append((bgt * wt_s[hh:hh + 1, :]).astype(BF16))
            lhs = jnp.concatenate([jnp.concatenate(m_parts, axis=1),
                                   jnp.concatenate(w_parts, axis=1)], axis=0)
            res = jnp.dot(lhs, rhs, preferred_element_type=F32)
            y_diag = res[0:q]
            s_new = res[q:2 * q]
            e_pair = jnp.where(lo_half, eacs_s[:, h0:h0 + 1], eacs_s[:, h0 + 1:h0 + 2])
            y_pair = (y_diag + e_pair * cs[:, pr * LANES:(pr + 1) * LANES]
                      + xs_f * dskip_ref[:, c0:c0 + LANES])
            cd_pair = jnp.where(lo_half_row, cd_s[:, h0:h0 + 1], cd_s[:, h0 + 1:h0 + 2])
            st[:, c0:c0 + LANES] = st[:, c0:c0 + LANES] * cd_pair + s_new
            ysc[:, c0:c0 + LANES] = y_pair
        zg = z_ref[:, g * gw:(g + 1) * gw].astype(F32)
        gt = ysc[:, g * gw:(g + 1) * gw] * (zg * _sigmoid(zg))
        ms = jnp.mean(gt * gt, axis=-1, keepdims=True)
        o_ref[:, g * gw:(g + 1) * gw] = (
            (gt * lax.rsqrt(ms + EPS)) * nw_ref[:, g * gw:(g + 1) * gw]).astype(BF16)


def _ssd(zx, dt_raw, dtt_raw, conv_w, conv_b, dt_bias, a_log, d_skip, norm_w, *, batch, seq, e_width):
    t = batch * seq
    gn = M_GROUPS * M_STATE
    conv_dim = e_width + 2 * gn
    q = SSD_CHUNK
    nc = seq // q
    n_heads = e_width // M_HEAD_DIM
    assert zx.shape == (t, e_width + conv_dim) and seq % q == 0
    assert e_width % gn == 0 and gn % LANES == 0

    def pad_row(v):
        return jnp.pad(v.astype(F32), (0, LANES - n_heads)).reshape(1, LANES)

    dtb_r = pad_row(dt_bias)
    alog_r = pad_row(a_log)
    dskip = jnp.repeat(d_skip.astype(F32), M_HEAD_DIM).reshape(1, e_width)
    rowblk = lambda b, c: b * nc + c
    const = lambda b, c: (0, 0)
    body = functools.partial(_ssd_body, e_width=e_width, gn=gn)
    return pl.pallas_call(
        body,
        grid=(batch, nc),
        in_specs=[
            pl.BlockSpec((q, e_width), lambda b, c: (rowblk(b, c), 0)),
            pl.BlockSpec((q, e_width), lambda b, c: (rowblk(b, c), 1)),
            pl.BlockSpec((q, gn), lambda b, c: (rowblk(b, c), 2 * e_width // gn)),
            pl.BlockSpec((q, gn), lambda b, c: (rowblk(b, c), 2 * e_width // gn + 1)),
            pl.BlockSpec((q, LANES), lambda b, c: (rowblk(b, c), 0)),
            pl.BlockSpec((LANES, q), lambda b, c: (0, rowblk(b, c))),
            pl.BlockSpec((M_CONV, conv_dim), const),
            pl.BlockSpec((1, conv_dim), const),
            pl.BlockSpec((1, LANES), const),
            pl.BlockSpec((LANES, 1), const),
            pl.BlockSpec((1, LANES), const),
            pl.BlockSpec((LANES, 1), const),
            pl.BlockSpec((1, e_width), const),
            pl.BlockSpec((1, e_width), const),
        ],
        out_specs=pl.BlockSpec((q, e_width), lambda b, c: (rowblk(b, c), 0)),
        out_shape=jax.ShapeDtypeStruct((t, e_width), BF16),
        scratch_shapes=[
            pltpu.VMEM((q + CONV_TAIL, conv_dim), BF16),
            pltpu.VMEM((q, conv_dim), F32),
            pltpu.VMEM((M_STATE, e_width), F32),
            pltpu.VMEM((q, e_width), F32),
            pltpu.VMEM((q, LANES), F32),
            pltpu.VMEM((LANES, q), F32),
            pltpu.VMEM((LANES, q), F32),
            pltpu.VMEM((q, LANES), F32),
            pltpu.VMEM((1, LANES), F32),
        ],
        compiler_params=pltpu.CompilerParams(
            dimension_semantics=("parallel", "arbitrary"), vmem_limit_bytes=VMEM_LIMIT),
        name="ssd_mixer",
    )(zx, zx, zx, zx, dt_raw, dtt_raw, conv_w.astype(F32), conv_b.astype(F32).reshape(1, conv_dim),
      dtb_r, dtb_r.reshape(LANES, 1), alog_r, alog_r.reshape(LANES, 1), dskip,
      norm_w.astype(F32).reshape(1, e_width))


def _out_proj_body(g_ref, w_ref, x_ref, fw_ref, o_ref, *, tn, final_norm):
    d = o_ref.shape[1]
    for c in range(d // tn):
        cols = slice(c * tn, (c + 1) * tn)
        o_ref[:, cols] = x_ref[:, cols] + jnp.dot(g_ref[...], w_ref[:, cols],
                                                  preferred_element_type=F32)
    if final_norm:
        r = o_ref[...]
        ms = jnp.mean(r * r, axis=-1, keepdims=True)
        o_ref[...] = (r * lax.rsqrt(ms + EPS)) * fw_ref[...]


def _out_proj(g, w, x2d, final_w, *, tm=512, tn=512):
    t, kdim = g.shape
    d = w.shape[1]
    assert t % tm == 0 and d % tn == 0
    final_norm = final_w is not None
    fw = (final_w if final_norm else jnp.ones((d,), F32)).astype(F32).reshape(1, d)
    body = functools.partial(_out_proj_body, tn=tn, final_norm=final_norm)
    return pl.pallas_call(
        body,
        grid=(t // tm,),
        in_specs=[
            pl.BlockSpec((tm, kdim), lambda i: (i, 0)),
            pl.BlockSpec((kdim, d), lambda i: (0, 0), pipeline_mode=pl.Buffered(1)),
            pl.BlockSpec((tm, d), lambda i: (i, 0)),
            pl.BlockSpec((1, d), lambda i: (0, 0)),
        ],
        out_specs=pl.BlockSpec((tm, d), lambda i: (i, 0)),
        out_shape=jax.ShapeDtypeStruct((t, d), F32),
        compiler_params=pltpu.CompilerParams(
            dimension_semantics=("parallel",), vmem_limit_bytes=VMEM_LIMIT),
        name="out_proj_final" if final_norm else "out_proj",
    )(g, w, x2d, fw)


GATE_COLS = 8
GATE_HEADS_PER_BLOCK = LANES // GATE_COLS


def _forget_cumsum_body(f_ref, b_r_ref, qx_ref, kx_ref, carry_r, *, blk, n_heads):
    i = pl.program_id(1)

    @pl.when(i == 0)
    def _():
        carry_r[...] = jnp.zeros(carry_r.shape, F32)

    li = lax.broadcasted_iota(jnp.int32, (blk, blk), 0)
    lj = lax.broadcasted_iota(jnp.int32, (blk, blk), 1)
    tril = (lj <= li).astype(F32)
    lf = _log_sigmoid(f_ref[...] + b_r_ref[...])
    cum = jnp.dot(tril, lf, preferred_element_type=F32, precision=HIGHEST) + carry_r[...]
    carry_r[...] = cum[blk - 1:blk, :]
    c2 = cum * LOG2E
    hi = c2.astype(BF16)
    r1 = c2 - hi.astype(F32)
    mid = r1.astype(BF16)
    lo = (r1 - mid.astype(F32)).astype(BF16)
    parts = jnp.concatenate([hi, mid, lo], axis=1)
    wide = 2 * LANES
    row = lax.broadcasted_iota(jnp.int32, (3 * LANES, wide), 0)
    col = lax.broadcasted_iota(jnp.int32, (3 * LANES, wide), 1)
    head = row & (LANES - 1)
    part = row >> 7
    live = head < n_heads
    e_q = jnp.where((col == head * GATE_COLS + part) & live, 1.0, 0.0).astype(BF16)
    e_k = jnp.where((col == head * GATE_COLS + 3 + part) & live, 1.0, 0.0).astype(BF16)
    lane = lax.broadcasted_iota(jnp.int32, (1, wide), 1)
    pos = lane & (GATE_COLS - 1)
    valid = (lane >> 3) < n_heads
    ones_q = jnp.where((pos >= 3) & (pos < 6) & valid, 1.0, 0.0)
    ones_k = jnp.where((pos < 3) & valid, 1.0, 0.0)
    qx = jnp.dot(parts, e_q, preferred_element_type=F32) + ones_q
    kx = ones_k - jnp.dot(parts, e_k, preferred_element_type=F32)
    for half in range(2):
        qx_ref[0, half] = qx[:, half * LANES:(half + 1) * LANES].astype(BF16)
        kx_ref[0, half] = kx[:, half * LANES:(half + 1) * LANES].astype(BF16)


def _forget_cumsum(f, b_forget, *, batch, seq, blk=512):
    nb = seq // blk
    n_heads = b_forget.shape[0]
    assert n_heads * GATE_COLS <= 2 * LANES
    b_r = jnp.pad(b_forget.astype(F32), (0, LANES - n_heads)).reshape(1, LANES)
    body = functools.partial(_forget_cumsum_body, blk=blk, n_heads=n_heads)
    xspec = pl.BlockSpec((1, 2, blk, LANES), lambda b, i: (b, 0, i, 0))
    xshape = jax.ShapeDtypeStruct((batch, 2, seq, LANES), BF16)
    return pl.pallas_call(
        body,
        grid=(batch, nb),
        in_specs=[
            pl.BlockSpec((blk, LANES), lambda b, i: (b * nb + i, 0)),
            pl.BlockSpec((1, LANES), lambda b, i: (0, 0)),
        ],
        out_specs=[xspec, xspec],
        out_shape=[xshape, xshape],
        scratch_shapes=[pltpu.VMEM((1, LANES), F32)],
        compiler_params=pltpu.CompilerParams(
            dimension_semantics=("parallel", "arbitrary"), vmem_limit_bytes=VMEM_LIMIT),
        name="forget_cumsum",
    )(f, b_r)


V_ROWS = F_HEAD_DIM + 16


def _fox_attn_body(q_ref, k_ref, v_ref, z_ref, qx_ref, kx_ref, o_ref,
                   kxm_scr, vt_scr, acc_scr, *, seq, tq, tk, wide_tk, hpb):
    pair = pl.program_id(1)
    nq = seq // tq
    assert tq == tk and seq % tq == 0

    lane = lax.broadcasted_iota(jnp.int32, (seq, LANES), 1)
    kxf = kx_ref[0, 0].astype(F32)
    r16 = lax.broadcasted_iota(jnp.int32, (V_ROWS - F_HEAD_DIM, seq), 0)
    ones_rows = jnp.where(r16 == 0, 1.0, 0.0).astype(BF16)
    for hh in range(hpb):
        mine = (lane >> 3) == ((pair * hpb + hh) % GATE_HEADS_PER_BLOCK)
        kxm_scr[hh] = jnp.where(mine, kxf, 0.0).astype(BF16)
        vt_scr[hh, F_HEAD_DIM:V_ROWS, :] = ones_rows

    kpos = lax.broadcasted_iota(jnp.int32, (tk, tq), 0)
    qpos = lax.broadcasted_iota(jnp.int32, (tk, tq), 1)
    causal = kpos <= qpos

    for qi in range(nq):
        qrows = pl.ds(qi * tq, tq)
        qa = []
        for hh in range(hpb):
            for c in range(tq // LANES):
                rows = pl.ds(qi * tq + c * LANES, LANES)
                vt_scr[hh, 0:F_HEAD_DIM, rows] = v_ref[hh, rows, :].astype(F32).T.astype(BF16)
            qcat = jnp.concatenate([q_ref[hh, qrows, :], qx_ref[0, 0, qrows, :]], axis=1)
            qa.append(qcat.astype(F32).T.astype(BF16))
        m = [None] * hpb
        segments = [(qi * tk, tk, True)]
        pos = 0
        while pos < qi * tk:
            width = min(wide_tk, qi * tk - pos)
            segments.append((pos, width, False))
            pos += width
        for si, (k0, kw, masked) in enumerate(segments):
            krows = pl.ds(k0, kw)
            for hh in range(hpb):
                ka = jnp.concatenate([k_ref[hh, krows, :], kxm_scr[hh, krows, :]], axis=1)
                st = jnp.dot(ka, qa[hh], preferred_element_type=F32)
                if masked:
                    st = jnp.where(causal, st, NEG_INF)
                mb = jnp.max(st, axis=0, keepdims=True)
                m_new = mb if si == 0 else jnp.maximum(m[hh], mb)
                pt = jnp.exp2(st - m_new).astype(BF16)
                pv = jnp.dot(vt_scr[hh, :, krows], pt, preferred_element_type=F32)
                if si == 0:
                    acc_scr[hh] = pv
                else:
                    acc_scr[hh] = acc_scr[hh] * jnp.exp2(m[hh] - m_new) + pv
                m[hh] = m_new
        for hh in range(hpb):
            ot = acc_scr[hh, 0:F_HEAD_DIM, :] / acc_scr[hh, F_HEAD_DIM:F_HEAD_DIM + 1, :]
            zv = z_ref[hh, qrows, :].astype(F32)
            o_ref[qrows, hh * F_HEAD_DIM:(hh + 1) * F_HEAD_DIM] = (
                ot.T * (zv * _sigmoid(zv))).astype(BF16)


def _fox_attention(p_hm, qx, kx, *, batch, seq, n_heads, tq=512, tk=512, wide_tk=1024, hpb=2):
    t = batch * seq
    e_width = n_heads * F_HEAD_DIM
    assert n_heads % hpb == 0 and GATE_HEADS_PER_BLOCK % hpb == 0 and wide_tk % tk == 0
    body = functools.partial(_fox_attn_body, seq=seq, tq=tq, tk=tk, wide_tk=wide_tk, hpb=hpb)
    hm = lambda off: pl.BlockSpec((hpb, seq, F_HEAD_DIM), lambda b, p: (off // hpb + p, b, 0))
    xspec = pl.BlockSpec((1, 1, seq, LANES), lambda b, p: (b, (p * hpb) // GATE_HEADS_PER_BLOCK, 0, 0))
    return pl.pallas_call(
        body,
        grid=(batch, n_heads // hpb),
        in_specs=[hm(0), hm(n_heads), hm(2 * n_heads), hm(3 * n_heads), xspec, xspec],
        out_specs=pl.BlockSpec((seq, hpb * F_HEAD_DIM), lambda b, p: (b, p)),
        out_shape=jax.ShapeDtypeStruct((t, e_width), BF16),
        scratch_shapes=[
            pltpu.VMEM((hpb, seq, LANES), BF16),
            pltpu.VMEM((hpb, V_ROWS, seq), BF16),
            pltpu.VMEM((hpb, V_ROWS, tq), F32),
        ],
        compiler_params=pltpu.CompilerParams(
            dimension_semantics=("parallel", "arbitrary"), vmem_limit_bytes=VMEM_LIMIT),
        name="fox_attention",
    )(p_hm, p_hm, p_hm, p_hm, qx, kx)


def kernel(x, norm_w, m_w_in, m_conv_w, m_conv_b, m_dt_bias, m_A_log, m_D, m_norm_w, m_w_out,
           f_w_in, f_b_forget, f_w_out, final_norm_w):
    batch, seq, d_model = x.shape
    t = batch * seq
    e_width = m_w_out.shape[1]
    m_heads = m_dt_bias.shape[1]
    f_heads = f_b_forget.shape[1]
    conv_dim = m_conv_w.shape[2]
    assert norm_w.shape[0] == 2 and m_w_in.shape[0] == 1 and f_w_in.shape[0] == 1
    assert e_width == m_heads * M_HEAD_DIM == f_heads * F_HEAD_DIM

    xf = x.reshape(t, d_model).astype(F32)

    w_in = m_w_in[0].T
    n_main = e_width + conv_dim
    w_dt = jnp.pad(w_in[n_main:, :], ((0, LANES - m_heads), (0, 0))).astype(BF16)
    zx, dt_raw, dtt_raw = _norm_matmul(xf, norm_w[0], w_in, n_main, w_dt, head_major=False,
                                       with_transposed=True)
    yg = _ssd(zx, dt_raw, dtt_raw, m_conv_w[0], m_conv_b[0], m_dt_bias[0], m_A_log[0], m_D[0],
              m_norm_w[0], batch=batch, seq=seq, e_width=e_width)
    x1 = _out_proj(yg, m_w_out[0].astype(BF16), xf, None)

    fw_in = f_w_in[0].T
    fw_f = jnp.pad(fw_in[4 * e_width:, :], ((0, LANES - f_heads), (0, 0))).astype(BF16)
    p_hm, f_raw = _norm_matmul(x1, norm_w[1], fw_in, 4 * e_width, fw_f, head_major=True,
                               with_transposed=False,
                               lead_cols=e_width, lead_scale=LOG2E / math.sqrt(F_HEAD_DIM))
    qx, kx = _forget_cumsum(f_raw, f_b_forget[0], batch=batch, seq=seq)
    og = _fox_attention(p_hm, qx, kx, batch=batch, seq=seq, n_heads=f_heads)
    out = _out_proj(og, f_w_out[0].astype(BF16), x1, final_norm_w)
    return out.reshape(batch, seq, d_model)
```

```python
import functools
import math

import jax
import jax.numpy as jnp
from jax import lax
from jax.experimental import pallas as pl
from jax.experimental.pallas import tpu as pltpu

F32 = jnp.float32
BF16 = jnp.bfloat16
EPS = 1e-6
LANES = 128
LOG2E = 1.4426950408889634
NEG_INF = float("-inf")
HIGHEST = lax.Precision.HIGHEST
VMEM_LIMIT = 56 * 1024 * 1024

M_HEAD_DIM = 64
M_GROUPS = 8
M_STATE = 128
M_CONV = 4
SSD_CHUNK = 128
CONV_TAIL = 16
F_HEAD_DIM = 128

_NT = (((1,), (1,)), ((), ()))


def _sigmoid(v):
    return 1.0 / (1.0 + jnp.exp(-v))


def _softplus(v):
    return jnp.maximum(v, 0.0) + jnp.log(1.0 + jnp.exp(-jnp.abs(v)))


def _log_sigmoid(v):
    return jnp.minimum(v, 0.0) - jnp.log1p(jnp.exp(-jnp.abs(v)))


def _norm_matmul_body(x_ref, nw_ref, wt_ref, wst_ref, o_ref, os_ref, *rest,
                      tm, tn, head_major, row_chunk, with_transposed, lead_tiles, lead_scale):
    if with_transposed:
        ost_ref, h_scr = rest
    else:
        (h_scr,) = rest
    j = pl.program_id(1)

    @pl.when(j == 0)
    def _():
        nw = nw_ref[...]

        def norm_rows(r, carry):
            rows = pl.ds(pl.multiple_of(r * row_chunk, row_chunk), row_chunk)
            xv = x_ref[rows, :]
            ms = jnp.mean(xv * xv, axis=-1, keepdims=True)
            h_scr[rows, :] = ((xv * lax.rsqrt(ms + EPS)) * nw).astype(BF16)
            return carry

        lax.fori_loop(0, tm // row_chunk, norm_rows, 0)
        small = lax.dot_general(h_scr[...], wst_ref[...], _NT, preferred_element_type=F32)
        os_ref[...] = small
        if with_transposed:
            ost_ref[...] = small.T

    res = lax.dot_general(h_scr[...], wt_ref[...].astype(BF16), _NT, preferred_element_type=F32)
    if lead_tiles:
        res = res * jnp.where(j < lead_tiles, jnp.float32(lead_scale), jnp.float32(1.0))
    res = res.astype(o_ref.dtype)
    if head_major:
        for c in range(tn // LANES):
            o_ref[c] = res[:, c * LANES:(c + 1) * LANES]
    else:
        o_ref[...] = res


def _norm_matmul(x2d, nw, wt_all, n, wt_small, *, head_major, with_transposed, lead_cols=0,
                 lead_scale=1.0, tm=1024, tn=1024):
    t, d = x2d.shape
    assert t % tm == 0 and n % tn == 0 and wt_small.shape == (LANES, d) and lead_cols % tn == 0
    assert wt_all.shape[1] == d and wt_all.shape[0] >= n
    grid = (t // tm, n // tn)
    if head_major:
        main_shape = jax.ShapeDtypeStruct((n // LANES, t, LANES), BF16)
        main_spec = pl.BlockSpec((tn // LANES, tm, LANES), lambda i, j: (j, i, 0))
    else:
        main_shape = jax.ShapeDtypeStruct((t, n), BF16)
        main_spec = pl.BlockSpec((tm, tn), lambda i, j: (i, j))
    out_specs = [main_spec, pl.BlockSpec((tm, LANES), lambda i, j: (i, 0))]
    out_shape = [main_shape, jax.ShapeDtypeStruct((t, LANES), F32)]
    if with_transposed:
        out_specs.append(pl.BlockSpec((LANES, tm), lambda i, j: (0, i)))
        out_shape.append(jax.ShapeDtypeStruct((LANES, t), F32))
    body = functools.partial(_norm_matmul_body, tm=tm, tn=tn, head_major=head_major, row_chunk=64,
                             with_transposed=with_transposed, lead_tiles=lead_cols // tn,
                             lead_scale=lead_scale)
    return pl.pallas_call(
        body,
        grid=grid,
        in_specs=[
            pl.BlockSpec((tm, d), lambda i, j: (i, 0)),
            pl.BlockSpec((1, d), lambda i, j: (0, 0)),
            pl.BlockSpec((tn, d), lambda i, j: (j, 0)),
            pl.BlockSpec((LANES, d), lambda i, j: (0, 0)),
        ],
        out_specs=out_specs,
        out_shape=out_shape,
        scratch_shapes=[pltpu.VMEM((tm, d), BF16)],
        compiler_params=pltpu.CompilerParams(
            dimension_semantics=("parallel", "arbitrary"), vmem_limit_bytes=VMEM_LIMIT),
        name="norm_in_proj_hm" if head_major else "norm_in_proj",
    )(x2d, nw.reshape(1, d), wt_all, wt_small)


def _ssd_body(z_ref, x_ref, b_ref, c_ref, dt_ref, dtt_ref, cw_ref, cbias_ref,
              dtb_r_ref, dtb_c_ref, alog_r_ref, alog_c_ref, dskip_ref, nw_ref,
              o_ref,
              ext, xc, st, ysc, acs_s, acst_s, wt_s, eacs_s, cd_s,
              *, e_width, gn):
    q = SSD_CHUNK
    conv_dim = e_width + 2 * gn
    cidx = pl.program_id(1)

    @pl.when(cidx == 0)
    def _():
        ext[0:CONV_TAIL, :] = jnp.zeros((CONV_TAIL, conv_dim), BF16)
        st[...] = jnp.zeros(st.shape, F32)

    ext[pl.ds(CONV_TAIL, q), 0:e_width] = x_ref[...]
    ext[pl.ds(CONV_TAIL, q), e_width:e_width + gn] = b_ref[...]
    ext[pl.ds(CONV_TAIL, q), e_width + gn:conv_dim] = c_ref[...]

    n_sh = M_CONV - 1
    prow = lax.broadcasted_iota(jnp.int32, (n_sh * q, q + CONV_TAIL), 0)
    pcol = lax.broadcasted_iota(jnp.int32, (n_sh * q, q + CONV_TAIL), 1)
    tap = prow >> 7
    shift_sel = jnp.where(pcol == (prow & (q - 1)) + tap + (CONV_TAIL - n_sh), 1.0, 0.0).astype(BF16)
    lane_chunk = 512
    for i in range(conv_dim // lane_chunk):
        lanes = slice(i * lane_chunk, (i + 1) * lane_chunk)
        shifted = jnp.dot(shift_sel, ext[:, lanes], preferred_element_type=F32)
        acc = cbias_ref[:, lanes] + cw_ref[n_sh:M_CONV, lanes] * ext[pl.ds(CONV_TAIL, q), lanes].astype(F32)
        for k in range(n_sh):
            acc = acc + cw_ref[k:k + 1, lanes] * shifted[k * q:(k + 1) * q]
        xc[:, lanes] = acc * _sigmoid(acc)
    ext[0:CONV_TAIL, :] = ext[pl.ds(q, CONV_TAIL), :]

    li = lax.broadcasted_iota(jnp.int32, (q, q), 0)
    lj = lax.broadcasted_iota(jnp.int32, (q, q), 1)
    causal = lj <= li
    tril = causal.astype(F32)
    triu = (li <= lj).astype(F32)

    dt = _softplus(dt_ref[...] + dtb_r_ref[...])
    dtt = _softplus(dtt_ref[...] + dtb_c_ref[...])
    a_r = -jnp.exp(alog_r_ref[...])
    a_c = -jnp.exp(alog_c_ref[...])
    acs = jnp.dot(tril, dt * a_r, preferred_element_type=F32, precision=HIGHEST)
    acst = jnp.dot(dtt * a_c, triu, preferred_element_type=F32, precision=HIGHEST)
    acs_s[...] = acs * LOG2E
    acst_s[...] = (acst - jnp.log(dtt)) * LOG2E
    wt_s[...] = jnp.exp(acst[:, q - 1:q] - acst) * dtt
    eacs_s[...] = jnp.exp(acs)
    cd_s[...] = jnp.exp(acs[q - 1:q, :])

    lane = lax.broadcasted_iota(jnp.int32, (q, LANES), 1)
    lo_half = lane < M_HEAD_DIM
    lane_row = lax.broadcasted_iota(jnp.int32, (1, LANES), 1)
    lo_half_row = lane_row < M_HEAD_DIM
    hpg = e_width // M_HEAD_DIM // M_GROUPS
    gw = hpg * M_HEAD_DIM

    for g in range(M_GROUPS):
        bg = xc[:, e_width + g * M_STATE:e_width + (g + 1) * M_STATE]
        cg = xc[:, e_width + gn + g * M_STATE:e_width + gn + (g + 1) * M_STATE]
        bgb = bg.astype(BF16)
        cgb = cg.astype(BF16)
        cb = lax.dot_general(cgb, bgb, _NT, preferred_element_type=F32)
        bgt = bg.T
        cs = jnp.dot(cgb, st[:, g * gw:(g + 1) * gw].astype(BF16),
                     preferred_element_type=F32)
        for pr in range(hpg // 2):
            h0 = g * hpg + 2 * pr
            c0 = g * gw + pr * LANES
            xs_f = xc[:, c0:c0 + LANES]
            xs_b = xs_f.astype(BF16)
            zero = jnp.zeros_like(xs_b)
            rhs = jnp.concatenate([jnp.where(lo_half, xs_b, zero),
                                   jnp.where(lo_half, zero, xs_b)], axis=0)
            m_parts = []
            w_parts = []
            for hh in (h0, h0 + 1):
                seg = acs_s[:, hh:hh + 1] - acst_s[hh:hh + 1, :]
                lm = jnp.exp2(jnp.where(causal, seg, NEG_INF))
                m_parts.append((cb * lm).astype(BF16))
                w_parts.append((bgt * wt_s[hh:hh + 1, :]).astype(BF16))
            lhs = jnp.concatenate([jnp.concatenate(m_parts, axis=1),
                                   jnp.concatenate(w_parts, axis=1)], axis=0)
            res = jnp.dot(lhs, rhs, preferred_element_type=F32)
            y_diag = res[0:q]
            s_new = res[q:2 * q]
            e_pair = jnp.where(lo_half, eacs_s[:, h0:h0 + 1], eacs_s[:, h0 + 1:h0 + 2])
            y_pair = (y_diag + e_pair * cs[:, pr * LANES:(pr + 1) * LANES]
                      + xs_f * dskip_ref[:, c0:c0 + LANES])
            cd_pair = jnp.where(lo_half_row, cd_s[:, h0:h0 + 1], cd_s[:, h0 + 1:h0 + 2])
            st[:, c0:c0 + LANES] = st[:, c0:c0 + LANES] * cd_pair + s_new
            ysc[:, c0:c0 + LANES] = y_pair
        zg = z_ref[:, g * gw:(g + 1) * gw].astype(F32)
        gt = ysc[:, g * gw:(g + 1) * gw] * (zg * _sigmoid(zg))
        ms = jnp.mean(gt * gt, axis=-1, keepdims=True)
        o_ref[:, g * gw:(g + 1) * gw] = (
            (gt * lax.rsqrt(ms + EPS)) * nw_ref[:, g * gw:(g + 1) * gw]).astype(BF16)


def _ssd(zx, dt_raw, dtt_raw, conv_w, conv_b, dt_bias, a_log, d_skip, norm_w, *, batch, seq, e_width):
    t = batch * seq
    gn = M_GROUPS * M_STATE
    conv_dim = e_width + 2 * gn
    q = SSD_CHUNK
    nc = seq // q
    n_heads = e_width // M_HEAD_DIM
    assert zx.shape == (t, e_width + conv_dim) and seq % q == 0
    assert e_width % gn == 0 and gn % LANES == 0

    def pad_row(v):
        return jnp.pad(v.astype(F32), (0, LANES - n_heads)).reshape(1, LANES)

    dtb_r = pad_row(dt_bias)
    alog_r = pad_row(a_log)
    dskip = jnp.repeat(d_skip.astype(F32), M_HEAD_DIM).reshape(1, e_width)
    rowblk = lambda b, c: b * nc + c
    const = lambda b, c: (0, 0)
    body = functools.partial(_ssd_body, e_width=e_width, gn=gn)
    return pl.pallas_call(
        body,
        grid=(batch, nc),
        in_specs=[
            pl.BlockSpec((q, e_width), lambda b, c: (rowblk(b, c), 0)),
            pl.BlockSpec((q, e_width), lambda b, c: (rowblk(b, c), 1)),
            pl.BlockSpec((q, gn), lambda b, c: (rowblk(b, c), 2 * e_width // gn)),
            pl.BlockSpec((q, gn), lambda b, c: (rowblk(b, c), 2 * e_width // gn + 1)),
            pl.BlockSpec((q, LANES), lambda b, c: (rowblk(b, c), 0)),
            pl.BlockSpec((LANES, q), lambda b, c: (0, rowblk(b, c))),
            pl.BlockSpec((M_CONV, conv_dim), const),
            pl.BlockSpec((1, conv_dim), const),
            pl.BlockSpec((1, LANES), const),
            pl.BlockSpec((LANES, 1), const),
            pl.BlockSpec((1, LANES), const),
            pl.BlockSpec((LANES, 1), const),
            pl.BlockSpec((1, e_width), const),
            pl.BlockSpec((1, e_width), const),
        ],
        out_specs=pl.BlockSpec((q, e_width), lambda b, c: (rowblk(b, c), 0)),
        out_shape=jax.ShapeDtypeStruct((t, e_width), BF16),
        scratch_shapes=[
            pltpu.VMEM((q + CONV_TAIL, conv_dim), BF16),
            pltpu.VMEM((q, conv_dim), F32),
            pltpu.VMEM((M_STATE, e_width), F32),
            pltpu.VMEM((q, e_width), F32),
            pltpu.VMEM((q, LANES), F32),
            pltpu.VMEM((LANES, q), F32),
            pltpu.VMEM((LANES, q), F32),
            pltpu.VMEM((q, LANES), F32),
            pltpu.VMEM((1, LANES), F32),
        ],
        compiler_params=pltpu.CompilerParams(
            dimension_semantics=("parallel", "arbitrary"), vmem_limit_bytes=VMEM_LIMIT),
        name="ssd_mixer",
    )(zx, zx, zx, zx, dt_raw, dtt_raw, conv_w.astype(F32), conv_b.astype(F32).reshape(1, conv_dim),
      dtb_r, dtb_r.reshape(LANES, 1), alog_r, alog_r.reshape(LANES, 1), dskip,
      norm_w.astype(F32).reshape(1, e_width))


def _out_proj_body(g_ref, w_ref, x_ref, fw_ref, o_ref, *, tn, final_norm):
    d = o_ref.shape[1]
    for c in range(d // tn):
        cols = slice(c * tn, (c + 1) * tn)
        o_ref[:, cols] = x_ref[:, cols] + jnp.dot(g_ref[...], w_ref[:, cols],
                                                  preferred_element_type=F32)
    if final_norm:
        r = o_ref[...]
        ms = jnp.mean(r * r, axis=-1, keepdims=True)
        o_ref[...] = (r * lax.rsqrt(ms + EPS)) * fw_ref[...]


def _out_proj(g, w, x2d, final_w, *, tm=512, tn=512):
    t, kdim = g.shape
    d = w.shape[1]
    assert t % tm == 0 and d % tn == 0
    final_norm = final_w is not None
    fw = (final_w if final_norm else jnp.ones((d,), F32)).astype(F32).reshape(1, d)
    body = functools.partial(_out_proj_body, tn=tn, final_norm=final_norm)
    return pl.pallas_call(
        body,
        grid=(t // tm,),
        in_specs=[
            pl.BlockSpec((tm, kdim), lambda i: (i, 0)),
            pl.BlockSpec((kdim, d), lambda i: (0, 0), pipeline_mode=pl.Buffered(1)),
            pl.BlockSpec((tm, d), lambda i: (i, 0)),
            pl.BlockSpec((1, d), lambda i: (0, 0)),
        ],
        out_specs=pl.BlockSpec((tm, d), lambda i: (i, 0)),
        out_shape=jax.ShapeDtypeStruct((t, d), F32),
        compiler_params=pltpu.CompilerParams(
            dimension_semantics=("parallel",), vmem_limit_bytes=VMEM_LIMIT),
        name="out_proj_final" if final_norm else "out_proj",
    )(g, w, x2d, fw)


GATE_COLS = 8
GATE_HEADS_PER_BLOCK = LANES // GATE_COLS


def _forget_cumsum_body(f_ref, b_r_ref, qx_ref, kx_ref, carry_r, *, blk, n_heads):
    i = pl.program_id(1)

    @pl.when(i == 0)
    def _():
        carry_r[...] = jnp.zeros(carry_r.shape, F32)

    li = lax.broadcasted_iota(jnp.int32, (blk, blk), 0)
    lj = lax.broadcasted_iota(jnp.int32, (blk, blk), 1)
    tril = (lj <= li).astype(F32)
    lf = _log_sigmoid(f_ref[...] + b_r_ref[...])
    cum = jnp.dot(tril, lf, preferred_element_type=F32, precision=HIGHEST) + carry_r[...]
    carry_r[...] = cum[blk - 1:blk, :]
    c2 = cum * LOG2E
    hi = c2.astype(BF16)
    r1 = c2 - hi.astype(F32)
    mid = r1.astype(BF16)
    lo = (r1 - mid.astype(F32)).astype(BF16)
    parts = jnp.concatenate([hi, mid, lo], axis=1)
    wide = 2 * LANES
    row = lax.broadcasted_iota(jnp.int32, (3 * LANES, wide), 0)
    col = lax.broadcasted_iota(jnp.int32, (3 * LANES, wide), 1)
    head = row & (LANES - 1)
    part = row >> 7
    live = head < n_heads
    e_q = jnp.where((col == head * GATE_COLS + part) & live, 1.0, 0.0).astype(BF16)
    e_k = jnp.where((col == head * GATE_COLS + 3 + part) & live, 1.0, 0.0).astype(BF16)
    lane = lax.broadcasted_iota(jnp.int32, (1, wide), 1)
    pos = lane & (GATE_COLS - 1)
    valid = (lane >> 3) < n_heads
    ones_q = jnp.where((pos >= 3) & (pos < 6) & valid, 1.0, 0.0)
    ones_k = jnp.where((pos < 3) & valid, 1.0, 0.0)
    qx = jnp.dot(parts, e_q, preferred_element_type=F32) + ones_q
    kx = ones_k - jnp.dot(parts, e_k, preferred_element_type=F32)
    for half in range(2):
        qx_ref[0, half] = qx[:, half * LANES:(half + 1) * LANES].astype(BF16)
        kx_ref[0, half] = kx[:, half * LANES:(half + 1) * LANES].astype(BF16)


def _forget_cumsum(f, b_forget, *, batch, seq, blk=512):
    nb = seq // blk
    n_heads = b_forget.shape[0]
    assert n_heads * GATE_COLS <= 2 * LANES
    b_r = jnp.pad(b_forget.astype(F32), (0, LANES - n_heads)).reshape(1, LANES)
    body = functools.partial(_forget_cumsum_body, blk=blk, n_heads=n_heads)
    xspec = pl.BlockSpec((1, 2, blk, LANES), lambda b, i: (b, 0, i, 0))
    xshape = jax.ShapeDtypeStruct((batch, 2, seq, LANES), BF16)
    return pl.pallas_call(
        body,
        grid=(batch, nb),
        in_specs=[
            pl.BlockSpec((blk, LANES), lambda b, i: (b * nb + i, 0)),
            pl.BlockSpec((1, LANES), lambda b, i: (0, 0)),
        ],
        out_specs=[xspec, xspec],
        out_shape=[xshape, xshape],
        scratch_shapes=[pltpu.VMEM((1, LANES), F32)],
        compiler_params=pltpu.CompilerParams(
            dimension_semantics=("parallel", "arbitrary"), vmem_limit_bytes=VMEM_LIMIT),
        name="forget_cumsum",
    )(f, b_r)


V_ROWS = F_HEAD_DIM + 16
ATTN_LOOKAHEAD = 2
ATTN_SLOTS = ATTN_LOOKAHEAD + 1


def _fox_attn_body(q_ref, k_ref, v_ref, z_ref, qx_ref, kx_ref, o_ref,
                   kxm_scr, vt_scr, acc_scr, st_scr, *, seq, tq, tk, wide_tk, hpb):
    pair = pl.program_id(1)
    nq = seq // tq
    half = tq // 2
    assert tq == tk and seq % tq == 0

    lane = lax.broadcasted_iota(jnp.int32, (seq, LANES), 1)
    kxf = kx_ref[0, 0].astype(F32)
    r16 = lax.broadcasted_iota(jnp.int32, (V_ROWS - F_HEAD_DIM, seq), 0)
    ones_rows = jnp.where(r16 == 0, 1.0, 0.0).astype(BF16)
    for hh in range(hpb):
        mine = (lane >> 3) == ((pair * hpb + hh) % GATE_HEADS_PER_BLOCK)
        kxm_scr[hh] = jnp.where(mine, kxf, 0.0).astype(BF16)
        vt_scr[hh, F_HEAD_DIM:V_ROWS, :] = ones_rows

    kpos = lax.broadcasted_iota(jnp.int32, (tk, tq), 0)
    qpos = lax.broadcasted_iota(jnp.int32, (tk, tq), 1)
    causal = kpos <= qpos

    items = []
    for qi in range(nq):
        units = [("diag", qi * tk, half, 0, half), ("diag", qi * tk, tq, half, tq)]
        pos = 0
        while pos < qi * tk:
            kw = min(wide_tk, qi * tk - pos)
            units.append(("below", pos, kw, 0, tq))
            pos += kw
        for ui, unit in enumerate(units):
            for hh in range(hpb):
                items.append((qi, hh, ui, len(units)) + unit)

    qat = {}
    m_run = {}
    m_diag = {}

    def setup(qi):
        qrows = pl.ds(qi * tq, tq)
        for hh in range(hpb):
            for c in range(tq // LANES):
                rows = pl.ds(qi * tq + c * LANES, LANES)
                vt_scr[hh, 0:F_HEAD_DIM, rows] = v_ref[hh, rows, :].astype(F32).T.astype(BF16)
            qcat = jnp.concatenate([q_ref[hh, qrows, :], qx_ref[0, 0, qrows, :]], axis=1)
            qat[(qi, hh)] = qcat.astype(F32).T.astype(BF16)

    def logits(item, slot):
        qi, hh, ui, nu, kind, k0, kw, c0, c1 = item
        if ui == 0 and hh == 0:
            setup(qi)
        krows = pl.ds(k0, kw)
        ka = jnp.concatenate([k_ref[hh, krows, :], kxm_scr[hh, krows, :]], axis=1)
        st = jnp.dot(ka, qat[(qi, hh)][:, c0:c1], preferred_element_type=F32)
        if kind == "diag":
            st = jnp.where(causal[0:kw, c0:c1], st, NEG_INF)
        st_scr[slot, 0:kw, 0:c1 - c0] = st

    def softmax_pv(item, slot):
        qi, hh, ui, nu, kind, k0, kw, c0, c1 = item
        krows = pl.ds(k0, kw)
        st = st_scr[slot, 0:kw, 0:c1 - c0]
        mb = jnp.max(st, axis=0, keepdims=True)
        if kind == "diag":
            pt = jnp.exp2(st - mb).astype(BF16)
            acc_scr[hh, :, c0:c1] = jnp.dot(vt_scr[hh, :, krows], pt, preferred_element_type=F32)
            m_diag.setdefault((qi, hh), []).append(mb)
            if c1 == tq:
                m_run[(qi, hh)] = jnp.concatenate(m_diag[(qi, hh)], axis=1)
        else:
            m_old = m_run[(qi, hh)]
            m_new = jnp.maximum(m_old, mb)
            pt = jnp.exp2(st - m_new).astype(BF16)
            pv = jnp.dot(vt_scr[hh, :, krows], pt, preferred_element_type=F32)
            acc_scr[hh] = acc_scr[hh] * jnp.exp2(m_old - m_new) + pv
            m_run[(qi, hh)] = m_new
        if ui == nu - 1:
            qrows = pl.ds(qi * tq, tq)
            ot = acc_scr[hh, 0:F_HEAD_DIM, :] / acc_scr[hh, F_HEAD_DIM:F_HEAD_DIM + 1, :]
            zv = z_ref[hh, qrows, :].astype(F32)
            o_ref[qrows, hh * F_HEAD_DIM:(hh + 1) * F_HEAD_DIM] = (
                ot.T * (zv * _sigmoid(zv))).astype(BF16)

    n_items = len(items)
    for i in range(min(ATTN_LOOKAHEAD, n_items)):
        logits(items[i], i % ATTN_SLOTS)
    for i in range(n_items):
        if i + ATTN_LOOKAHEAD < n_items:
            logits(items[i + ATTN_LOOKAHEAD], (i + ATTN_LOOKAHEAD) % ATTN_SLOTS)
        softmax_pv(items[i], i % ATTN_SLOTS)


def _fox_attention(p_hm, qx, kx, *, batch, seq, n_heads, tq=512, tk=512, wide_tk=1024, hpb=2):
    t = batch * seq
    e_width = n_heads * F_HEAD_DIM
    assert n_heads % hpb == 0 and GATE_HEADS_PER_BLOCK % hpb == 0 and wide_tk % tk == 0
    body = functools.partial(_fox_attn_body, seq=seq, tq=tq, tk=tk, wide_tk=wide_tk, hpb=hpb)
    hm = lambda off: pl.BlockSpec((hpb, seq, F_HEAD_DIM), lambda b, p: (off // hpb + p, b, 0))
    xspec = pl.BlockSpec((1, 1, seq, LANES), lambda b, p: (b, (p * hpb) // GATE_HEADS_PER_BLOCK, 0, 0))
    return pl.pallas_call(
        body,
        grid=(batch, n_heads // hpb),
        in_specs=[hm(0), hm(n_heads), hm(2 * n_heads), hm(3 * n_heads), xspec, xspec],
        out_specs=pl.BlockSpec((seq, hpb * F_HEAD_DIM), lambda b, p: (b, p)),
        out_shape=jax.ShapeDtypeStruct((t, e_width), BF16),
        scratch_shapes=[
            pltpu.VMEM((hpb, seq, LANES), BF16),
            pltpu.VMEM((hpb, V_ROWS, seq), BF16),
            pltpu.VMEM((hpb, V_ROWS, tq), F32),
            pltpu.VMEM((ATTN_SLOTS, wide_tk, tq), F32),
        ],
        compiler_params=pltpu.CompilerParams(
            dimension_semantics=("parallel", "arbitrary"), vmem_limit_bytes=VMEM_LIMIT),
        name="fox_attention",
    )(p_hm, p_hm, p_hm, p_hm, qx, kx)


def kernel(x, norm_w, m_w_in, m_conv_w, m_conv_b, m_dt_bias, m_A_log, m_D, m_norm_w, m_w_out,
           f_w_in, f_b_forget, f_w_out, final_norm_w):
    batch, seq, d_model = x.shape
    t = batch * seq
    e_width = m_w_out.shape[1]
    m_heads = m_dt_bias.shape[1]
    f_heads = f_b_forget.shape[1]
    conv_dim = m_conv_w.shape[2]
    assert norm_w.shape[0] == 2 and m_w_in.shape[0] == 1 and f_w_in.shape[0] == 1
    assert e_width == m_heads * M_HEAD_DIM == f_heads * F_HEAD_DIM

    xf = x.reshape(t, d_model).astype(F32)

    w_in = m_w_in[0].T
    n_main = e_width + conv_dim
    w_dt = jnp.pad(w_in[n_main:, :], ((0, LANES - m_heads), (0, 0))).astype(BF16)
    zx, dt_raw, dtt_raw = _norm_matmul(xf, norm_w[0], w_in, n_main, w_dt, head_major=False,
                                       with_transposed=True)
    yg = _ssd(zx, dt_raw, dtt_raw, m_conv_w[0], m_conv_b[0], m_dt_bias[0], m_A_log[0], m_D[0],
              m_norm_w[0], batch=batch, seq=seq, e_width=e_width)
    x1 = _out_proj(yg, m_w_out[0].astype(BF16), xf, None)

    fw_in = f_w_in[0].T
    fw_f = jnp.pad(fw_in[4 * e_width:, :], ((0, LANES - f_heads), (0, 0))).astype(BF16)
    p_hm, f_raw = _norm_matmul(x1, norm_w[1], fw_in, 4 * e_width, fw_f, head_major=True,
                               with_transposed=False,
                               lead_cols=e_width, lead_scale=LOG2E / math.sqrt(F_HEAD_DIM))
    qx, kx = _forget_cumsum(f_raw, f_b_forget[0], batch=batch, seq=seq)
    og = _fox_attention(p_hm, qx, kx, batch=batch, seq=seq, n_heads=f_heads)
    out = _out_proj(og, f_w_out[0].astype(BF16), x1, final_norm_w)
    return out.reshape(batch, seq, d_model)
```

```python
import functools
import math

import jax
import jax.numpy as jnp
from jax import lax
from jax.experimental import pallas as pl
from jax.experimental.pallas import tpu as pltpu

F32 = jnp.float32
BF16 = jnp.bfloat16
EPS = 1e-6
LANES = 128
LOG2E = 1.4426950408889634
NEG_INF = float("-inf")
HIGHEST = lax.Precision.HIGHEST
VMEM_LIMIT = 56 * 1024 * 1024

M_HEAD_DIM = 64
M_GROUPS = 8
M_STATE = 128
M_CONV = 4
SSD_CHUNK = 128
CONV_TAIL = 16
SSD_LANE_CHUNK = 512
F_HEAD_DIM = 128

_NT = (((1,), (1,)), ((), ()))


def _sigmoid(v):
    return 1.0 / (1.0 + jnp.exp(-v))


def _softplus(v):
    return jnp.maximum(v, 0.0) + jnp.log(1.0 + jnp.exp(-jnp.abs(v)))


def _log_sigmoid(v):
    return jnp.minimum(v, 0.0) - jnp.log1p(jnp.exp(-jnp.abs(v)))


def _norm_matmul_body(x_ref, nw_ref, wt_ref, wst_ref, o_ref, os_ref, *rest,
                      tm, tn, head_major, row_chunk, with_transposed, lead_tiles, lead_scale):
    if with_transposed:
        ost_ref, h_scr = rest
    else:
        (h_scr,) = rest
    j = pl.program_id(1)

    @pl.when(j == 0)
    def _():
        nw = nw_ref[...]

        def norm_rows(r, carry):
            rows = pl.ds(pl.multiple_of(r * row_chunk, row_chunk), row_chunk)
            xv = x_ref[rows, :]
            ms = jnp.mean(xv * xv, axis=-1, keepdims=True)
            h_scr[rows, :] = ((xv * lax.rsqrt(ms + EPS)) * nw).astype(BF16)
            return carry

        lax.fori_loop(0, tm // row_chunk, norm_rows, 0, unroll=4)
        small = lax.dot_general(h_scr[...], wst_ref[...], _NT, preferred_element_type=F32)
        os_ref[...] = small
        if with_transposed:
            ost_ref[...] = small.T

    res = lax.dot_general(h_scr[...], wt_ref[...].astype(BF16), _NT, preferred_element_type=F32)
    if lead_tiles:
        res = res * jnp.where(j < lead_tiles, jnp.float32(lead_scale), jnp.float32(1.0))
    res = res.astype(o_ref.dtype)
    if head_major:
        for c in range(tn // LANES):
            o_ref[c] = res[:, c * LANES:(c + 1) * LANES]
    else:
        o_ref[...] = res


def _norm_matmul(x2d, nw, wt_all, n, wt_small, *, head_major, with_transposed, lead_cols=0,
                 lead_scale=1.0, tm=1024, tn=1024):
    t, d = x2d.shape
    assert t % tm == 0 and n % tn == 0 and wt_small.shape == (LANES, d) and lead_cols % tn == 0
    assert wt_all.shape[1] == d and wt_all.shape[0] >= n
    grid = (t // tm, n // tn)
    if head_major:
        main_shape = jax.ShapeDtypeStruct((n // LANES, t, LANES), BF16)
        main_spec = pl.BlockSpec((tn // LANES, tm, LANES), lambda i, j: (j, i, 0))
    else:
        main_shape = jax.ShapeDtypeStruct((t, n), BF16)
        main_spec = pl.BlockSpec((tm, tn), lambda i, j: (i, j))
    out_specs = [main_spec, pl.BlockSpec((tm, LANES), lambda i, j: (i, 0))]
    out_shape = [main_shape, jax.ShapeDtypeStruct((t, LANES), F32)]
    if with_transposed:
        out_specs.append(pl.BlockSpec((LANES, tm), lambda i, j: (0, i)))
        out_shape.append(jax.ShapeDtypeStruct((LANES, t), F32))
    body = functools.partial(_norm_matmul_body, tm=tm, tn=tn, head_major=head_major, row_chunk=64,
                             with_transposed=with_transposed, lead_tiles=lead_cols // tn,
                             lead_scale=lead_scale)
    return pl.pallas_call(
        body,
        grid=grid,
        in_specs=[
            pl.BlockSpec((tm, d), lambda i, j: (i, 0)),
            pl.BlockSpec((1, d), lambda i, j: (0, 0)),
            pl.BlockSpec((tn, d), lambda i, j: (j, 0)),
            pl.BlockSpec((LANES, d), lambda i, j: (0, 0)),
        ],
        out_specs=out_specs,
        out_shape=out_shape,
        scratch_shapes=[pltpu.VMEM((tm, d), BF16)],
        compiler_params=pltpu.CompilerParams(
            dimension_semantics=("parallel", "arbitrary"), vmem_limit_bytes=VMEM_LIMIT),
        name="norm_in_proj_hm" if head_major else "norm_in_proj",
    )(x2d, nw.reshape(1, d), wt_all, wt_small)


def _ssd_body(z_ref, x_ref, b_ref, c_ref, dt_ref, dtt_ref, cw_ref, cbias_ref,
              dtb_r_ref, dtb_c_ref, alog_r_ref, alog_c_ref, dskip_ref, nw_ref,
              o_ref,
              ext, xc, st, ysc, acs_s, acst_s, wt_s, eacs_s, cd_s,
              *, e_width, gn):
    q = SSD_CHUNK
    conv_dim = e_width + 2 * gn
    cidx = pl.program_id(1)

    @pl.when(cidx == 0)
    def _():
        ext[0:CONV_TAIL, :] = jnp.zeros((CONV_TAIL, conv_dim), BF16)
        st[...] = jnp.zeros(st.shape, F32)

    ext[pl.ds(CONV_TAIL, q), 0:e_width] = x_ref[...]
    ext[pl.ds(CONV_TAIL, q), e_width:e_width + gn] = b_ref[...]
    ext[pl.ds(CONV_TAIL, q), e_width + gn:conv_dim] = c_ref[...]

    n_sh = M_CONV - 1
    prow = lax.broadcasted_iota(jnp.int32, (n_sh * q, q + CONV_TAIL), 0)
    pcol = lax.broadcasted_iota(jnp.int32, (n_sh * q, q + CONV_TAIL), 1)
    tap = prow >> 7
    shift_sel = jnp.where(pcol == (prow & (q - 1)) + tap + (CONV_TAIL - n_sh), 1.0, 0.0).astype(BF16)
    lane_chunk = SSD_LANE_CHUNK
    for i in range(conv_dim // lane_chunk):
        lanes = slice(i * lane_chunk, (i + 1) * lane_chunk)
        shifted = jnp.dot(shift_sel, ext[:, lanes], preferred_element_type=F32)
        acc = cbias_ref[:, lanes] + cw_ref[n_sh:M_CONV, lanes] * ext[pl.ds(CONV_TAIL, q), lanes].astype(F32)
        for k in range(n_sh):
            acc = acc + cw_ref[k:k + 1, lanes] * shifted[k * q:(k + 1) * q]
        xc[:, lanes] = acc * _sigmoid(acc)
    ext[0:CONV_TAIL, :] = ext[pl.ds(q, CONV_TAIL), :]

    li = lax.broadcasted_iota(jnp.int32, (q, q), 0)
    lj = lax.broadcasted_iota(jnp.int32, (q, q), 1)
    causal = lj <= li
    tril = causal.astype(F32)
    triu = (li <= lj).astype(F32)

    dt = _softplus(dt_ref[...] + dtb_r_ref[...])
    dtt = _softplus(dtt_ref[...] + dtb_c_ref[...])
    a_r = -jnp.exp(alog_r_ref[...])
    a_c = -jnp.exp(alog_c_ref[...])
    acs = jnp.dot(tril, dt * a_r, preferred_element_type=F32, precision=HIGHEST)
    acst = jnp.dot(dtt * a_c, triu, preferred_element_type=F32, precision=HIGHEST)
    acs_s[...] = acs * LOG2E
    acst_s[...] = (acst - jnp.log(dtt)) * LOG2E
    wt_s[...] = jnp.exp(acst[:, q - 1:q] - acst) * dtt
    eacs_s[...] = jnp.exp(acs)
    cd_s[...] = jnp.exp(acs[q - 1:q, :])

    lane = lax.broadcasted_iota(jnp.int32, (q, LANES), 1)
    lo_half = lane < M_HEAD_DIM
    lane_row = lax.broadcasted_iota(jnp.int32, (1, LANES), 1)
    lo_half_row = lane_row < M_HEAD_DIM
    hpg = e_width // M_HEAD_DIM // M_GROUPS
    gw = hpg * M_HEAD_DIM

    for g in range(M_GROUPS):
        bg = xc[:, e_width + g * M_STATE:e_width + (g + 1) * M_STATE]
        cg = xc[:, e_width + gn + g * M_STATE:e_width + gn + (g + 1) * M_STATE]
        bgb = bg.astype(BF16)
        cgb = cg.astype(BF16)
        cb = lax.dot_general(cgb, bgb, _NT, preferred_element_type=F32)
        bgt = bg.T
        cs = jnp.dot(cgb, st[:, g * gw:(g + 1) * gw].astype(BF16),
                     preferred_element_type=F32)
        for pr in range(hpg // 2):
            h0 = g * hpg + 2 * pr
            c0 = g * gw + pr * LANES
            xs_f = xc[:, c0:c0 + LANES]
            xs_b = xs_f.astype(BF16)
            zero = jnp.zeros_like(xs_b)
            rhs = jnp.concatenate([jnp.where(lo_half, xs_b, zero),
                                   jnp.where(lo_half, zero, xs_b)], axis=0)
            m_parts = []
            w_parts = []
            for hh in (h0, h0 + 1):
                seg = acs_s[:, hh:hh + 1] - acst_s[hh:hh + 1, :]
                lm = jnp.exp2(jnp.where(causal, seg, NEG_INF))
                m_parts.append((cb * lm).astype(BF16))
                w_parts.append((bgt * wt_s[hh:hh + 1, :]).astype(BF16))
            lhs = jnp.concatenate([jnp.concatenate(m_parts, axis=1),
                                   jnp.concatenate(w_parts, axis=1)], axis=0)
            res = jnp.dot(lhs, rhs, preferred_element_type=F32)
            y_diag = res[0:q]
            s_new = res[q:2 * q]
            e_pair = jnp.where(lo_half, eacs_s[:, h0:h0 + 1], eacs_s[:, h0 + 1:h0 + 2])
            y_pair = (y_diag + e_pair * cs[:, pr * LANES:(pr + 1) * LANES]
                      + xs_f * dskip_ref[:, c0:c0 + LANES])
            cd_pair = jnp.where(lo_half_row, cd_s[:, h0:h0 + 1], cd_s[:, h0 + 1:h0 + 2])
            st[:, c0:c0 + LANES] = st[:, c0:c0 + LANES] * cd_pair + s_new
            ysc[:, c0:c0 + LANES] = y_pair
        zg = z_ref[:, g * gw:(g + 1) * gw].astype(F32)
        gt = ysc[:, g * gw:(g + 1) * gw] * (zg * _sigmoid(zg))
        ms = jnp.mean(gt * gt, axis=-1, keepdims=True)
        o_ref[:, g * gw:(g + 1) * gw] = (
            (gt * lax.rsqrt(ms + EPS)) * nw_ref[:, g * gw:(g + 1) * gw]).astype(BF16)


def _ssd(zx, dt_raw, dtt_raw, conv_w, conv_b, dt_bias, a_log, d_skip, norm_w, *, batch, seq, e_width):
    t = batch * seq
    gn = M_GROUPS * M_STATE
    conv_dim = e_width + 2 * gn
    q = SSD_CHUNK
    nc = seq // q
    n_heads = e_width // M_HEAD_DIM
    assert zx.shape == (t, e_width + conv_dim) and seq % q == 0
    assert e_width % gn == 0 and gn % LANES == 0

    def pad_row(v):
        return jnp.pad(v.astype(F32), (0, LANES - n_heads)).reshape(1, LANES)

    dtb_r = pad_row(dt_bias)
    alog_r = pad_row(a_log)
    dskip = jnp.repeat(d_skip.astype(F32), M_HEAD_DIM).reshape(1, e_width)
    rowblk = lambda b, c: b * nc + c
    const = lambda b, c: (0, 0)
    body = functools.partial(_ssd_body, e_width=e_width, gn=gn)
    return pl.pallas_call(
        body,
        grid=(batch, nc),
        in_specs=[
            pl.BlockSpec((q, e_width), lambda b, c: (rowblk(b, c), 0)),
            pl.BlockSpec((q, e_width), lambda b, c: (rowblk(b, c), 1)),
            pl.BlockSpec((q, gn), lambda b, c: (rowblk(b, c), 2 * e_width // gn)),
            pl.BlockSpec((q, gn), lambda b, c: (rowblk(b, c), 2 * e_width // gn + 1)),
            pl.BlockSpec((q, LANES), lambda b, c: (rowblk(b, c), 0)),
            pl.BlockSpec((LANES, q), lambda b, c: (0, rowblk(b, c))),
            pl.BlockSpec((M_CONV, conv_dim), const),
            pl.BlockSpec((1, conv_dim), const),
            pl.BlockSpec((1, LANES), const),
            pl.BlockSpec((LANES, 1), const),
            pl.BlockSpec((1, LANES), const),
            pl.BlockSpec((LANES, 1), const),
            pl.BlockSpec((1, e_width), const),
            pl.BlockSpec((1, e_width), const),
        ],
        out_specs=pl.BlockSpec((q, e_width), lambda b, c: (rowblk(b, c), 0)),
        out_shape=jax.ShapeDtypeStruct((t, e_width), BF16),
        scratch_shapes=[
            pltpu.VMEM((q + CONV_TAIL, conv_dim), BF16),
            pltpu.VMEM((q, conv_dim), F32),
            pltpu.VMEM((M_STATE, e_width), F32),
            pltpu.VMEM((q, e_width), F32),
            pltpu.VMEM((q, LANES), F32),
            pltpu.VMEM((LANES, q), F32),
            pltpu.VMEM((LANES, q), F32),
            pltpu.VMEM((q, LANES), F32),
            pltpu.VMEM((1, LANES), F32),
        ],
        compiler_params=pltpu.CompilerParams(
            dimension_semantics=("parallel", "arbitrary"), vmem_limit_bytes=VMEM_LIMIT),
        name="ssd_mixer",
    )(zx, zx, zx, zx, dt_raw, dtt_raw, conv_w.astype(F32), conv_b.astype(F32).reshape(1, conv_dim),
      dtb_r, dtb_r.reshape(LANES, 1), alog_r, alog_r.reshape(LANES, 1), dskip,
      norm_w.astype(F32).reshape(1, e_width))


def _out_proj_body(g_ref, w_ref, x_ref, fw_ref, o_ref, *, tn, final_norm):
    d = o_ref.shape[1]
    for c in range(d // tn):
        cols = slice(c * tn, (c + 1) * tn)
        o_ref[:, cols] = x_ref[:, cols] + jnp.dot(g_ref[...], w_ref[:, cols],
                                                  preferred_element_type=F32)
    if final_norm:
        r = o_ref[...]
        ms = jnp.mean(r * r, axis=-1, keepdims=True)
        o_ref[...] = (r * lax.rsqrt(ms + EPS)) * fw_ref[...]


def _out_proj(g, w, x2d, final_w, *, tm=512, tn=512):
    t, kdim = g.shape
    d = w.shape[1]
    assert t % tm == 0 and d % tn == 0
    final_norm = final_w is not None
    fw = (final_w if final_norm else jnp.ones((d,), F32)).astype(F32).reshape(1, d)
    body = functools.partial(_out_proj_body, tn=tn, final_norm=final_norm)
    return pl.pallas_call(
        body,
        grid=(t // tm,),
        in_specs=[
            pl.BlockSpec((tm, kdim), lambda i: (i, 0)),
            pl.BlockSpec((kdim, d), lambda i: (0, 0), pipeline_mode=pl.Buffered(1)),
            pl.BlockSpec((tm, d), lambda i: (i, 0)),
            pl.BlockSpec((1, d), lambda i: (0, 0)),
        ],
        out_specs=pl.BlockSpec((tm, d), lambda i: (i, 0)),
        out_shape=jax.ShapeDtypeStruct((t, d), F32),
        compiler_params=pltpu.CompilerParams(
            dimension_semantics=("parallel",), vmem_limit_bytes=VMEM_LIMIT),
        name="out_proj_final" if final_norm else "out_proj",
    )(g, w, x2d, fw)


GATE_COLS = 8
GATE_HEADS_PER_BLOCK = LANES // GATE_COLS


def _forget_cumsum_body(f_ref, b_r_ref, qx_ref, kx_ref, carry_r, *, blk, n_heads):
    i = pl.program_id(1)

    @pl.when(i == 0)
    def _():
        carry_r[...] = jnp.zeros(carry_r.shape, F32)

    li = lax.broadcasted_iota(jnp.int32, (blk, blk), 0)
    lj = lax.broadcasted_iota(jnp.int32, (blk, blk), 1)
    tril = (lj <= li).astype(F32)
    lf = _log_sigmoid(f_ref[...] + b_r_ref[...])
    cum = jnp.dot(tril, lf, preferred_element_type=F32, precision=HIGHEST) + carry_r[...]
    carry_r[...] = cum[blk - 1:blk, :]
    c2 = cum * LOG2E
    hi = c2.astype(BF16)
    r1 = c2 - hi.astype(F32)
    mid = r1.astype(BF16)
    lo = (r1 - mid.astype(F32)).astype(BF16)
    parts = jnp.concatenate([hi, mid, lo], axis=1)
    wide = 2 * LANES
    row = lax.broadcasted_iota(jnp.int32, (3 * LANES, wide), 0)
    col = lax.broadcasted_iota(jnp.int32, (3 * LANES, wide), 1)
    head = row & (LANES - 1)
    part = row >> 7
    live = head < n_heads
    e_q = jnp.where((col == head * GATE_COLS + part) & live, 1.0, 0.0).astype(BF16)
    e_k = jnp.where((col == head * GATE_COLS + 3 + part) & live, 1.0, 0.0).astype(BF16)
    lane = lax.broadcasted_iota(jnp.int32, (1, wide), 1)
    pos = lane & (GATE_COLS - 1)
    valid = (lane >> 3) < n_heads
    ones_q = jnp.where((pos >= 3) & (pos < 6) & valid, 1.0, 0.0)
    ones_k = jnp.where((pos < 3) & valid, 1.0, 0.0)
    qx = jnp.dot(parts, e_q, preferred_element_type=F32) + ones_q
    kx = ones_k - jnp.dot(parts, e_k, preferred_element_type=F32)
    for half in range(2):
        qx_ref[0, half] = qx[:, half * LANES:(half + 1) * LANES].astype(BF16)
        kx_ref[0, half] = kx[:, half * LANES:(half + 1) * LANES].astype(BF16)


def _forget_cumsum(f, b_forget, *, batch, seq, blk=512):
    nb = seq // blk
    n_heads = b_forget.shape[0]
    assert n_heads * GATE_COLS <= 2 * LANES
    b_r = jnp.pad(b_forget.astype(F32), (0, LANES - n_heads)).reshape(1, LANES)
    body = functools.partial(_forget_cumsum_body, blk=blk, n_heads=n_heads)
    xspec = pl.BlockSpec((1, 2, blk, LANES), lambda b, i: (b, 0, i, 0))
    xshape = jax.ShapeDtypeStruct((batch, 2, seq, LANES), BF16)
    return pl.pallas_call(
        body,
        grid=(batch, nb),
        in_specs=[
            pl.BlockSpec((blk, LANES), lambda b, i: (b * nb + i, 0)),
            pl.BlockSpec((1, LANES), lambda b, i: (0, 0)),
        ],
        out_specs=[xspec, xspec],
        out_shape=[xshape, xshape],
        scratch_shapes=[pltpu.VMEM((1, LANES), F32)],
        compiler_params=pltpu.CompilerParams(
            dimension_semantics=("parallel", "arbitrary"), vmem_limit_bytes=VMEM_LIMIT),
        name="forget_cumsum",
    )(f, b_r)


V_ROWS = F_HEAD_DIM + 16
ATTN_LOOKAHEAD = 2
ATTN_SLOTS = ATTN_LOOKAHEAD + 1


def _fox_attn_body(q_ref, k_ref, v_ref, z_ref, qx_ref, kx_ref, o_ref,
                   kxm_scr, vt_scr, acc_scr, st_scr, *, seq, tq, tk, wide_tk, hpb):
    pair = pl.program_id(1)
    nq = seq // tq
    half = tq // 2
    assert tq == tk and seq % tq == 0

    lane = lax.broadcasted_iota(jnp.int32, (seq, LANES), 1)
    kxf = kx_ref[0, 0].astype(F32)
    r16 = lax.broadcasted_iota(jnp.int32, (V_ROWS - F_HEAD_DIM, seq), 0)
    ones_rows = jnp.where(r16 == 0, 1.0, 0.0).astype(BF16)
    for hh in range(hpb):
        mine = (lane >> 3) == ((pair * hpb + hh) % GATE_HEADS_PER_BLOCK)
        kxm_scr[hh] = jnp.where(mine, kxf, 0.0).astype(BF16)
        vt_scr[hh, F_HEAD_DIM:V_ROWS, :] = ones_rows

    kpos = lax.broadcasted_iota(jnp.int32, (tk, tq), 0)
    qpos = lax.broadcasted_iota(jnp.int32, (tk, tq), 1)
    causal = kpos <= qpos

    items = []
    for qi in range(nq):
        units = [("diag", qi * tk, half, 0, half), ("diag", qi * tk, tq, half, tq)]
        pos = 0
        while pos < qi * tk:
            kw = min(wide_tk, qi * tk - pos)
            units.append(("below", pos, kw, 0, tq))
            pos += kw
        for ui, unit in enumerate(units):
            for hh in range(hpb):
                items.append((qi, hh, ui, len(units)) + unit)

    qat = {}
    m_run = {}
    m_diag = {}

    def setup(qi):
        qrows = pl.ds(qi * tq, tq)
        for hh in range(hpb):
            for c in range(tq // LANES):
                rows = pl.ds(qi * tq + c * LANES, LANES)
                vt_scr[hh, 0:F_HEAD_DIM, rows] = v_ref[hh, rows, :].astype(F32).T.astype(BF16)
            qcat = jnp.concatenate([q_ref[hh, qrows, :], qx_ref[0, 0, qrows, :]], axis=1)
            qat[(qi, hh)] = qcat.astype(F32).T.astype(BF16)

    def logits(item, slot):
        qi, hh, ui, nu, kind, k0, kw, c0, c1 = item
        if ui == 0 and hh == 0:
            setup(qi)
        krows = pl.ds(k0, kw)
        ka = jnp.concatenate([k_ref[hh, krows, :], kxm_scr[hh, krows, :]], axis=1)
        st = jnp.dot(ka, qat[(qi, hh)][:, c0:c1], preferred_element_type=F32)
        if kind == "diag":
            st = jnp.where(causal[0:kw, c0:c1], st, NEG_INF)
        st_scr[slot, 0:kw, 0:c1 - c0] = st

    def softmax_pv(item, slot):
        qi, hh, ui, nu, kind, k0, kw, c0, c1 = item
        krows = pl.ds(k0, kw)
        st = st_scr[slot, 0:kw, 0:c1 - c0]
        mb = jnp.max(st, axis=0, keepdims=True)
        if kind == "diag":
            pt = jnp.exp2(st - mb).astype(BF16)
            acc_scr[hh, :, c0:c1] = jnp.dot(vt_scr[hh, :, krows], pt, preferred_element_type=F32)
            m_diag.setdefault((qi, hh), []).append(mb)
            if c1 == tq:
                m_run[(qi, hh)] = jnp.concatenate(m_diag[(qi, hh)], axis=1)
        else:
            m_old = m_run[(qi, hh)]
            m_new = jnp.maximum(m_old, mb)
            pt = jnp.exp2(st - m_new).astype(BF16)
            pv = jnp.dot(vt_scr[hh, :, krows], pt, preferred_element_type=F32)
            acc_scr[hh] = acc_scr[hh] * jnp.exp2(m_old - m_new) + pv
            m_run[(qi, hh)] = m_new
        if ui == nu - 1:
            qrows = pl.ds(qi * tq, tq)
            ot = acc_scr[hh, 0:F_HEAD_DIM, :] / acc_scr[hh, F_HEAD_DIM:F_HEAD_DIM + 1, :]
            zv = z_ref[hh, qrows, :].astype(F32)
            o_ref[qrows, hh * F_HEAD_DIM:(hh + 1) * F_HEAD_DIM] = (
                ot.T * (zv * _sigmoid(zv))).astype(BF16)

    n_items = len(items)
    for i in range(min(ATTN_LOOKAHEAD, n_items)):
        logits(items[i], i % ATTN_SLOTS)
    for i in range(n_items):
        if i + ATTN_LOOKAHEAD < n_items:
            logits(items[i + ATTN_LOOKAHEAD], (i + ATTN_LOOKAHEAD) % ATTN_SLOTS)
        softmax_pv(items[i], i % ATTN_SLOTS)


def _fox_attention(p_hm, qx, kx, *, batch, seq, n_heads, tq=512, tk=512, wide_tk=1024, hpb=2):
    t = batch * seq
    e_width = n_heads * F_HEAD_DIM
    assert n_heads % hpb == 0 and GATE_HEADS_PER_BLOCK % hpb == 0 and wide_tk % tk == 0
    body = functools.partial(_fox_attn_body, seq=seq, tq=tq, tk=tk, wide_tk=wide_tk, hpb=hpb)
    hm = lambda off: pl.BlockSpec((hpb, seq, F_HEAD_DIM), lambda b, p: (off // hpb + p, b, 0))
    xspec = pl.BlockSpec((1, 1, seq, LANES), lambda b, p: (b, (p * hpb) // GATE_HEADS_PER_BLOCK, 0, 0))
    return pl.pallas_call(
        body,
        grid=(batch, n_heads // hpb),
        in_specs=[hm(0), hm(n_heads), hm(2 * n_heads), hm(3 * n_heads), xspec, xspec],
        out_specs=pl.BlockSpec((seq, hpb * F_HEAD_DIM), lambda b, p: (b, p)),
        out_shape=jax.ShapeDtypeStruct((t, e_width), BF16),
        scratch_shapes=[
            pltpu.VMEM((hpb, seq, LANES), BF16),
            pltpu.VMEM((hpb, V_ROWS, seq), BF16),
            pltpu.VMEM((hpb, V_ROWS, tq), F32),
            pltpu.VMEM((ATTN_SLOTS, wide_tk, tq), F32),
        ],
        compiler_params=pltpu.CompilerParams(
            dimension_semantics=("parallel", "arbitrary"), vmem_limit_bytes=VMEM_LIMIT),
        name="fox_attention",
    )(p_hm, p_hm, p_hm, p_hm, qx, kx)


def kernel(x, norm_w, m_w_in, m_conv_w, m_conv_b, m_dt_bias, m_A_log, m_D, m_norm_w, m_w_out,
           f_w_in, f_b_forget, f_w_out, final_norm_w):
    batch, seq, d_model = x.shape
    t = batch * seq
    e_width = m_w_out.shape[1]
    m_heads = m_dt_bias.shape[1]
    f_heads = f_b_forget.shape[1]
    conv_dim = m_conv_w.shape[2]
    assert norm_w.shape[0] == 2 and m_w_in.shape[0] == 1 and f_w_in.shape[0] == 1
    assert e_width == m_heads * M_HEAD_DIM == f_heads * F_HEAD_DIM

    xf = x.reshape(t, d_model).astype(F32)

    w_in = m_w_in[0].T
    n_main = e_width + conv_dim
    w_dt = jnp.pad(w_in[n_main:, :], ((0, LANES - m_heads), (0, 0))).astype(BF16)
    zx, dt_raw, dtt_raw = _norm_matmul(xf, norm_w[0], w_in, n_main, w_dt, head_major=False,
                                       with_transposed=True)
    yg = _ssd(zx, dt_raw, dtt_raw, m_conv_w[0], m_conv_b[0], m_dt_bias[0], m_A_log[0], m_D[0],
              m_norm_w[0], batch=batch, seq=seq, e_width=e_width)
    x1 = _out_proj(yg, m_w_out[0].astype(BF16), xf, None)

    fw_in = f_w_in[0].T
    fw_f = jnp.pad(fw_in[4 * e_width:, :], ((0, LANES - f_heads), (0, 0))).astype(BF16)
    p_hm, f_raw = _norm_matmul(x1, norm_w[1], fw_in, 4 * e_width, fw_f, head_major=True,
                               with_transposed=False,
                               lead_cols=e_width, lead_scale=LOG2E / math.sqrt(F_HEAD_DIM))
    qx, kx = _forget_cumsum(f_raw, f_b_forget[0], batch=batch, seq=seq)
    og = _fox_attention(p_hm, qx, kx, batch=batch, seq=seq, n_heads=f_heads)
    out = _out_proj(og, f_w_out[0].astype(BF16), x1, final_norm_w)
    return out.reshape(batch, seq, d_model)
```

```python
import functools
import math

import jax
import jax.numpy as jnp
from jax import lax
from jax.experimental import pallas as pl
from jax.experimental.pallas import tpu as pltpu

F32 = jnp.float32
BF16 = jnp.bfloat16
EPS = 1e-6
LANES = 128
LOG2E = 1.4426950408889634
NEG_INF = float("-inf")
HIGHEST = lax.Precision.HIGHEST
VMEM_LIMIT = 56 * 1024 * 1024

M_HEAD_DIM = 64
M_GROUPS = 8
M_STATE = 128
M_CONV = 4
SSD_CHUNK = 128
CONV_TAIL = 16
SSD_LANE_CHUNK = 512
F_HEAD_DIM = 128

_NT = (((1,), (1,)), ((), ()))


def _sigmoid(v):
    return 1.0 / (1.0 + jnp.exp(-v))


def _softplus(v):
    return jnp.maximum(v, 0.0) + jnp.log(1.0 + jnp.exp(-jnp.abs(v)))


def _log_sigmoid(v):
    return jnp.minimum(v, 0.0) - jnp.log1p(jnp.exp(-jnp.abs(v)))


def _norm_matmul_body(x_ref, nw_ref, wt_ref, wst_ref, o_ref, os_ref, *rest,
                      tm, tn, head_major, row_chunk, with_transposed, lead_tiles, lead_scale):
    if with_transposed:
        ost_ref, h_scr = rest
    else:
        (h_scr,) = rest
    j = pl.program_id(1)

    @pl.when(j == 0)
    def _():
        nw = nw_ref[...]

        def norm_rows(r, carry):
            rows = pl.ds(pl.multiple_of(r * row_chunk, row_chunk), row_chunk)
            xv = x_ref[rows, :]
            ms = jnp.mean(xv * xv, axis=-1, keepdims=True)
            h_scr[rows, :] = ((xv * lax.rsqrt(ms + EPS)) * nw).astype(BF16)
            return carry

        lax.fori_loop(0, tm // row_chunk, norm_rows, 0, unroll=4)
        small = lax.dot_general(h_scr[...], wst_ref[...], _NT, preferred_element_type=F32)
        os_ref[...] = small
        if with_transposed:
            ost_ref[...] = small.T

    res = lax.dot_general(h_scr[...], wt_ref[...].astype(BF16), _NT, preferred_element_type=F32)
    if lead_tiles:
        res = res * jnp.where(j < lead_tiles, jnp.float32(lead_scale), jnp.float32(1.0))
    res = res.astype(o_ref.dtype)
    if head_major:
        for c in range(tn // LANES):
            o_ref[c] = res[:, c * LANES:(c + 1) * LANES]
    else:
        o_ref[...] = res


def _norm_matmul(x2d, nw, wt_all, n, wt_small, *, head_major, with_transposed, lead_cols=0,
                 lead_scale=1.0, tm=1024, tn=1024):
    t, d = x2d.shape
    assert t % tm == 0 and n % tn == 0 and wt_small.shape == (LANES, d) and lead_cols % tn == 0
    assert wt_all.shape[1] == d and wt_all.shape[0] >= n
    grid = (t // tm, n // tn)
    if head_major:
        main_shape = jax.ShapeDtypeStruct((n // LANES, t, LANES), BF16)
        main_spec = pl.BlockSpec((tn // LANES, tm, LANES), lambda i, j: (j, i, 0))
    else:
        main_shape = jax.ShapeDtypeStruct((t, n), BF16)
        main_spec = pl.BlockSpec((tm, tn), lambda i, j: (i, j))
    out_specs = [main_spec, pl.BlockSpec((tm, LANES), lambda i, j: (i, 0))]
    out_shape = [main_shape, jax.ShapeDtypeStruct((t, LANES), F32)]
    if with_transposed:
        out_specs.append(pl.BlockSpec((LANES, tm), lambda i, j: (0, i)))
        out_shape.append(jax.ShapeDtypeStruct((LANES, t), F32))
    body = functools.partial(_norm_matmul_body, tm=tm, tn=tn, head_major=head_major, row_chunk=64,
                             with_transposed=with_transposed, lead_tiles=lead_cols // tn,
                             lead_scale=lead_scale)
    return pl.pallas_call(
        body,
        grid=grid,
        in_specs=[
            pl.BlockSpec((tm, d), lambda i, j: (i, 0)),
            pl.BlockSpec((1, d), lambda i, j: (0, 0)),
            pl.BlockSpec((tn, d), lambda i, j: (j, 0)),
            pl.BlockSpec((LANES, d), lambda i, j: (0, 0)),
        ],
        out_specs=out_specs,
        out_shape=out_shape,
        scratch_shapes=[pltpu.VMEM((tm, d), BF16)],
        compiler_params=pltpu.CompilerParams(
            dimension_semantics=("parallel", "arbitrary"), vmem_limit_bytes=VMEM_LIMIT),
        name="norm_in_proj_hm" if head_major else "norm_in_proj",
    )(x2d, nw.reshape(1, d), wt_all, wt_small)


def _ssd_body(z_ref, x_ref, b_ref, c_ref, dt_ref, dtt_ref, cw_ref, cbias_ref,
              dtb_r_ref, dtb_c_ref, alog_r_ref, alog_c_ref, dskip_ref, nw_ref,
              o_ref,
              ext, xc, st, ysc, acs_s, acst_s, wt_s, eacs_s, cd_s,
              *, e_width, gn):
    q = SSD_CHUNK
    conv_dim = e_width + 2 * gn
    cidx = pl.program_id(1)

    @pl.when(cidx == 0)
    def _():
        ext[0:CONV_TAIL, :] = jnp.zeros((CONV_TAIL, conv_dim), BF16)
        st[...] = jnp.zeros(st.shape, F32)

    ext[pl.ds(CONV_TAIL, q), 0:e_width] = x_ref[...]
    ext[pl.ds(CONV_TAIL, q), e_width:e_width + gn] = b_ref[...]
    ext[pl.ds(CONV_TAIL, q), e_width + gn:conv_dim] = c_ref[...]

    n_sh = M_CONV - 1
    prow = lax.broadcasted_iota(jnp.int32, (n_sh * q, q + CONV_TAIL), 0)
    pcol = lax.broadcasted_iota(jnp.int32, (n_sh * q, q + CONV_TAIL), 1)
    assert q & (q - 1) == 0
    tap = prow >> (q.bit_length() - 1)
    shift_sel = jnp.where(pcol == (prow & (q - 1)) + tap + (CONV_TAIL - n_sh), 1.0, 0.0).astype(BF16)
    lane_chunk = SSD_LANE_CHUNK
    for i in range(conv_dim // lane_chunk):
        lanes = slice(i * lane_chunk, (i + 1) * lane_chunk)
        shifted = jnp.dot(shift_sel, ext[:, lanes], preferred_element_type=F32)
        acc = cbias_ref[:, lanes] + cw_ref[n_sh:M_CONV, lanes] * ext[pl.ds(CONV_TAIL, q), lanes].astype(F32)
        for k in range(n_sh):
            acc = acc + cw_ref[k:k + 1, lanes] * shifted[k * q:(k + 1) * q]
        xc[:, lanes] = acc * _sigmoid(acc)
    ext[0:CONV_TAIL, :] = ext[pl.ds(q, CONV_TAIL), :]

    li = lax.broadcasted_iota(jnp.int32, (q, q), 0)
    lj = lax.broadcasted_iota(jnp.int32, (q, q), 1)
    causal = lj <= li
    tril = causal.astype(F32)
    triu = (li <= lj).astype(F32)

    dt = _softplus(dt_ref[...] + dtb_r_ref[...])
    dtt = _softplus(dtt_ref[...] + dtb_c_ref[...])
    a_r = -jnp.exp(alog_r_ref[...])
    a_c = -jnp.exp(alog_c_ref[...])
    acs = jnp.dot(tril, dt * a_r, preferred_element_type=F32, precision=HIGHEST)
    acst = jnp.dot(dtt * a_c, triu, preferred_element_type=F32, precision=HIGHEST)
    acs_s[...] = acs * LOG2E
    acst_s[...] = (acst - jnp.log(dtt)) * LOG2E
    wt_s[...] = jnp.exp(acst[:, q - 1:q] - acst) * dtt
    eacs_s[...] = jnp.exp(acs)
    cd_s[...] = jnp.exp(acs[q - 1:q, :])

    lane = lax.broadcasted_iota(jnp.int32, (q, LANES), 1)
    lo_half = lane < M_HEAD_DIM
    lane_row = lax.broadcasted_iota(jnp.int32, (1, LANES), 1)
    lo_half_row = lane_row < M_HEAD_DIM
    hpg = e_width // M_HEAD_DIM // M_GROUPS
    gw = hpg * M_HEAD_DIM

    for g in range(M_GROUPS):
        bg = xc[:, e_width + g * M_STATE:e_width + (g + 1) * M_STATE]
        cg = xc[:, e_width + gn + g * M_STATE:e_width + gn + (g + 1) * M_STATE]
        bgb = bg.astype(BF16)
        cgb = cg.astype(BF16)
        cb = lax.dot_general(cgb, bgb, _NT, preferred_element_type=F32)
        bgt = bg.T
        cs = jnp.dot(cgb, st[:, g * gw:(g + 1) * gw].astype(BF16),
                     preferred_element_type=F32)
        for pr in range(hpg // 2):
            h0 = g * hpg + 2 * pr
            c0 = g * gw + pr * LANES
            xs_f = xc[:, c0:c0 + LANES]
            xs_b = xs_f.astype(BF16)
            zero = jnp.zeros_like(xs_b)
            rhs = jnp.concatenate([jnp.where(lo_half, xs_b, zero),
                                   jnp.where(lo_half, zero, xs_b)], axis=0)
            m_parts = []
            w_parts = []
            for hh in (h0, h0 + 1):
                seg = acs_s[:, hh:hh + 1] - acst_s[hh:hh + 1, :]
                lm = jnp.exp2(jnp.where(causal, seg, NEG_INF))
                m_parts.append((cb * lm).astype(BF16))
                w_parts.append((bgt * wt_s[hh:hh + 1, :]).astype(BF16))
            lhs = jnp.concatenate([jnp.concatenate(m_parts, axis=1),
                                   jnp.concatenate(w_parts, axis=1)], axis=0)
            res = jnp.dot(lhs, rhs, preferred_element_type=F32)
            y_diag = res[0:q]
            s_new = res[q:2 * q]
            e_pair = jnp.where(lo_half, eacs_s[:, h0:h0 + 1], eacs_s[:, h0 + 1:h0 + 2])
            y_pair = (y_diag + e_pair * cs[:, pr * LANES:(pr + 1) * LANES]
                      + xs_f * dskip_ref[:, c0:c0 + LANES])
            cd_pair = jnp.where(lo_half_row, cd_s[:, h0:h0 + 1], cd_s[:, h0 + 1:h0 + 2])
            st[:, c0:c0 + LANES] = st[:, c0:c0 + LANES] * cd_pair + s_new
            ysc[:, c0:c0 + LANES] = y_pair
        zg = z_ref[:, g * gw:(g + 1) * gw].astype(F32)
        gt = ysc[:, g * gw:(g + 1) * gw] * (zg * _sigmoid(zg))
        ms = jnp.mean(gt * gt, axis=-1, keepdims=True)
        o_ref[:, g * gw:(g + 1) * gw] = (
            (gt * lax.rsqrt(ms + EPS)) * nw_ref[:, g * gw:(g + 1) * gw]).astype(BF16)


def _ssd(zx, dt_raw, dtt_raw, conv_w, conv_b, dt_bias, a_log, d_skip, norm_w, *, batch, seq, e_width):
    t = batch * seq
    gn = M_GROUPS * M_STATE
    conv_dim = e_width + 2 * gn
    q = SSD_CHUNK
    nc = seq // q
    n_heads = e_width // M_HEAD_DIM
    assert zx.shape == (t, e_width + conv_dim) and seq % q == 0
    assert e_width % gn == 0 and gn % LANES == 0

    def pad_row(v):
        return jnp.pad(v.astype(F32), (0, LANES - n_heads)).reshape(1, LANES)

    dtb_r = pad_row(dt_bias)
    alog_r = pad_row(a_log)
    dskip = jnp.repeat(d_skip.astype(F32), M_HEAD_DIM).reshape(1, e_width)
    rowblk = lambda b, c: b * nc + c
    const = lambda b, c: (0, 0)
    body = functools.partial(_ssd_body, e_width=e_width, gn=gn)
    return pl.pallas_call(
        body,
        grid=(batch, nc),
        in_specs=[
            pl.BlockSpec((q, e_width), lambda b, c: (rowblk(b, c), 0)),
            pl.BlockSpec((q, e_width), lambda b, c: (rowblk(b, c), 1)),
            pl.BlockSpec((q, gn), lambda b, c: (rowblk(b, c), 2 * e_width // gn)),
            pl.BlockSpec((q, gn), lambda b, c: (rowblk(b, c), 2 * e_width // gn + 1)),
            pl.BlockSpec((q, LANES), lambda b, c: (rowblk(b, c), 0)),
            pl.BlockSpec((LANES, q), lambda b, c: (0, rowblk(b, c))),
            pl.BlockSpec((M_CONV, conv_dim), const),
            pl.BlockSpec((1, conv_dim), const),
            pl.BlockSpec((1, LANES), const),
            pl.BlockSpec((LANES, 1), const),
            pl.BlockSpec((1, LANES), const),
            pl.BlockSpec((LANES, 1), const),
            pl.BlockSpec((1, e_width), const),
            pl.BlockSpec((1, e_width), const),
        ],
        out_specs=pl.BlockSpec((q, e_width), lambda b, c: (rowblk(b, c), 0)),
        out_shape=jax.ShapeDtypeStruct((t, e_width), BF16),
        scratch_shapes=[
            pltpu.VMEM((q + CONV_TAIL, conv_dim), BF16),
            pltpu.VMEM((q, conv_dim), F32),
            pltpu.VMEM((M_STATE, e_width), F32),
            pltpu.VMEM((q, e_width), F32),
            pltpu.VMEM((q, LANES), F32),
            pltpu.VMEM((LANES, q), F32),
            pltpu.VMEM((LANES, q), F32),
            pltpu.VMEM((q, LANES), F32),
            pltpu.VMEM((1, LANES), F32),
        ],
        compiler_params=pltpu.CompilerParams(
            dimension_semantics=("parallel", "arbitrary"), vmem_limit_bytes=VMEM_LIMIT),
        name="ssd_mixer",
    )(zx, zx, zx, zx, dt_raw, dtt_raw, conv_w.astype(F32), conv_b.astype(F32).reshape(1, conv_dim),
      dtb_r, dtb_r.reshape(LANES, 1), alog_r, alog_r.reshape(LANES, 1), dskip,
      norm_w.astype(F32).reshape(1, e_width))


def _out_proj_body(g_ref, w_ref, x_ref, fw_ref, o_ref, *, tn, final_norm):
    d = o_ref.shape[1]
    for c in range(d // tn):
        cols = slice(c * tn, (c + 1) * tn)
        o_ref[:, cols] = x_ref[:, cols] + jnp.dot(g_ref[...], w_ref[:, cols],
                                                  preferred_element_type=F32)
    if final_norm:
        r = o_ref[...]
        ms = jnp.mean(r * r, axis=-1, keepdims=True)
        o_ref[...] = (r * lax.rsqrt(ms + EPS)) * fw_ref[...]


def _out_proj(g, w, x2d, final_w, *, tm=512, tn=512):
    t, kdim = g.shape
    d = w.shape[1]
    assert t % tm == 0 and d % tn == 0
    final_norm = final_w is not None
    fw = (final_w if final_norm else jnp.ones((d,), F32)).astype(F32).reshape(1, d)
    body = functools.partial(_out_proj_body, tn=tn, final_norm=final_norm)
    return pl.pallas_call(
        body,
        grid=(t // tm,),
        in_specs=[
            pl.BlockSpec((tm, kdim), lambda i: (i, 0)),
            pl.BlockSpec((kdim, d), lambda i: (0, 0), pipeline_mode=pl.Buffered(1)),
            pl.BlockSpec((tm, d), lambda i: (i, 0)),
            pl.BlockSpec((1, d), lambda i: (0, 0)),
        ],
        out_specs=pl.BlockSpec((tm, d), lambda i: (i, 0)),
        out_shape=jax.ShapeDtypeStruct((t, d), F32),
        compiler_params=pltpu.CompilerParams(
            dimension_semantics=("parallel",), vmem_limit_bytes=VMEM_LIMIT),
        name="out_proj_final" if final_norm else "out_proj",
    )(g, w, x2d, fw)


GATE_COLS = 8
GATE_HEADS_PER_BLOCK = LANES // GATE_COLS


def _forget_cumsum_body(f_ref, b_r_ref, qx_ref, kx_ref, carry_r, *, blk, n_heads):
    i = pl.program_id(1)

    @pl.when(i == 0)
    def _():
        carry_r[...] = jnp.zeros(carry_r.shape, F32)

    li = lax.broadcasted_iota(jnp.int32, (blk, blk), 0)
    lj = lax.broadcasted_iota(jnp.int32, (blk, blk), 1)
    tril = (lj <= li).astype(F32)
    lf = _log_sigmoid(f_ref[...] + b_r_ref[...])
    cum = jnp.dot(tril, lf, preferred_element_type=F32, precision=HIGHEST) + carry_r[...]
    carry_r[...] = cum[blk - 1:blk, :]
    c2 = cum * LOG2E
    hi = c2.astype(BF16)
    r1 = c2 - hi.astype(F32)
    mid = r1.astype(BF16)
    lo = (r1 - mid.astype(F32)).astype(BF16)
    parts = jnp.concatenate([hi, mid, lo], axis=1)
    wide = 2 * LANES
    row = lax.broadcasted_iota(jnp.int32, (3 * LANES, wide), 0)
    col = lax.broadcasted_iota(jnp.int32, (3 * LANES, wide), 1)
    head = row & (LANES - 1)
    part = row >> 7
    live = head < n_heads
    e_q = jnp.where((col == head * GATE_COLS + part) & live, 1.0, 0.0).astype(BF16)
    e_k = jnp.where((col == head * GATE_COLS + 3 + part) & live, 1.0, 0.0).astype(BF16)
    lane = lax.broadcasted_iota(jnp.int32, (1, wide), 1)
    pos = lane & (GATE_COLS - 1)
    valid = (lane >> 3) < n_heads
    ones_q = jnp.where((pos >= 3) & (pos < 6) & valid, 1.0, 0.0)
    ones_k = jnp.where((pos < 3) & valid, 1.0, 0.0)
    qx = jnp.dot(parts, e_q, preferred_element_type=F32) + ones_q
    kx = ones_k - jnp.dot(parts, e_k, preferred_element_type=F32)
    for half in range(2):
        qx_ref[0, half] = qx[:, half * LANES:(half + 1) * LANES].astype(BF16)
        kx_ref[0, half] = kx[:, half * LANES:(half + 1) * LANES].astype(BF16)


def _forget_cumsum(f, b_forget, *, batch, seq, blk=512):
    nb = seq // blk
    n_heads = b_forget.shape[0]
    assert n_heads * GATE_COLS <= 2 * LANES
    b_r = jnp.pad(b_forget.astype(F32), (0, LANES - n_heads)).reshape(1, LANES)
    body = functools.partial(_forget_cumsum_body, blk=blk, n_heads=n_heads)
    xspec = pl.BlockSpec((1, 2, blk, LANES), lambda b, i: (b, 0, i, 0))
    xshape = jax.ShapeDtypeStruct((batch, 2, seq, LANES), BF16)
    return pl.pallas_call(
        body,
        grid=(batch, nb),
        in_specs=[
            pl.BlockSpec((blk, LANES), lambda b, i: (b * nb + i, 0)),
            pl.BlockSpec((1, LANES), lambda b, i: (0, 0)),
        ],
        out_specs=[xspec, xspec],
        out_shape=[xshape, xshape],
        scratch_shapes=[pltpu.VMEM((1, LANES), F32)],
        compiler_params=pltpu.CompilerParams(
            dimension_semantics=("parallel", "arbitrary"), vmem_limit_bytes=VMEM_LIMIT),
        name="forget_cumsum",
    )(f, b_r)


V_ROWS = F_HEAD_DIM + 16
ATTN_LOOKAHEAD = 2
ATTN_SLOTS = ATTN_LOOKAHEAD + 1


def _fox_attn_body(q_ref, k_ref, v_ref, z_ref, qx_ref, kx_ref, o_ref,
                   kxm_scr, vt_scr, acc_scr, st_scr, *, seq, tq, tk, wide_tk, hpb):
    pair = pl.program_id(1)
    nq = seq // tq
    half = tq // 2
    assert tq == tk and seq % tq == 0

    lane = lax.broadcasted_iota(jnp.int32, (seq, LANES), 1)
    kxf = kx_ref[0, 0].astype(F32)
    r16 = lax.broadcasted_iota(jnp.int32, (V_ROWS - F_HEAD_DIM, seq), 0)
    ones_rows = jnp.where(r16 == 0, 1.0, 0.0).astype(BF16)
    for hh in range(hpb):
        mine = (lane >> 3) == ((pair * hpb + hh) % GATE_HEADS_PER_BLOCK)
        kxm_scr[hh] = jnp.where(mine, kxf, 0.0).astype(BF16)
        vt_scr[hh, F_HEAD_DIM:V_ROWS, :] = ones_rows

    kpos = lax.broadcasted_iota(jnp.int32, (tk, tq), 0)
    qpos = lax.broadcasted_iota(jnp.int32, (tk, tq), 1)
    causal = kpos <= qpos

    items = []
    for qi in range(nq):
        units = [("diag", qi * tk, half, 0, half), ("diag", qi * tk, tq, half, tq)]
        pos = 0
        while pos < qi * tk:
            kw = min(wide_tk, qi * tk - pos)
            units.append(("below", pos, kw, 0, tq))
            pos += kw
        for ui, unit in enumerate(units):
            for hh in range(hpb):
                items.append((qi, hh, ui, len(units)) + unit)

    qat = {}
    m_run = {}
    m_diag = {}

    def setup(qi):
        qrows = pl.ds(qi * tq, tq)
        for hh in range(hpb):
            for c in range(tq // LANES):
                rows = pl.ds(qi * tq + c * LANES, LANES)
                vt_scr[hh, 0:F_HEAD_DIM, rows] = v_ref[hh, rows, :].astype(F32).T.astype(BF16)
            qcat = jnp.concatenate([q_ref[hh, qrows, :], qx_ref[0, 0, qrows, :]], axis=1)
            qat[(qi, hh)] = qcat.astype(F32).T.astype(BF16)

    def logits(item, slot):
        qi, hh, ui, nu, kind, k0, kw, c0, c1 = item
        if ui == 0 and hh == 0:
            setup(qi)
        krows = pl.ds(k0, kw)
        ka = jnp.concatenate([k_ref[hh, krows, :], kxm_scr[hh, krows, :]], axis=1)
        st = jnp.dot(ka, qat[(qi, hh)][:, c0:c1], preferred_element_type=F32)
        if kind == "diag":
            st = jnp.where(causal[0:kw, c0:c1], st, NEG_INF)
        st_scr[slot, 0:kw, 0:c1 - c0] = st

    def softmax_pv(item, slot):
        qi, hh, ui, nu, kind, k0, kw, c0, c1 = item
        krows = pl.ds(k0, kw)
        st = st_scr[slot, 0:kw, 0:c1 - c0]
        mb = jnp.max(st, axis=0, keepdims=True)
        if kind == "diag":
            pt = jnp.exp2(st - mb).astype(BF16)
            acc_scr[hh, :, c0:c1] = jnp.dot(vt_scr[hh, :, krows], pt, preferred_element_type=F32)
            m_diag.setdefault((qi, hh), []).append(mb)
            if c1 == tq:
                m_run[(qi, hh)] = jnp.concatenate(m_diag[(qi, hh)], axis=1)
        else:
            m_old = m_run[(qi, hh)]
            m_new = jnp.maximum(m_old, mb)
            pt = jnp.exp2(st - m_new).astype(BF16)
            pv = jnp.dot(vt_scr[hh, :, krows], pt, preferred_element_type=F32)
            acc_scr[hh] = acc_scr[hh] * jnp.exp2(m_old - m_new) + pv
            m_run[(qi, hh)] = m_new
        if ui == nu - 1:
            qrows = pl.ds(qi * tq, tq)
            ot = acc_scr[hh, 0:F_HEAD_DIM, :] / acc_scr[hh, F_HEAD_DIM:F_HEAD_DIM + 1, :]
            zv = z_ref[hh, qrows, :].astype(F32)
            o_ref[qrows, hh * F_HEAD_DIM:(hh + 1) * F_HEAD_DIM] = (
                ot.T * (zv * _sigmoid(zv))).astype(BF16)

    n_items = len(items)
    for i in range(min(ATTN_LOOKAHEAD, n_items)):
        logits(items[i], i % ATTN_SLOTS)
    for i in range(n_items):
        if i + ATTN_LOOKAHEAD < n_items:
            logits(items[i + ATTN_LOOKAHEAD], (i + ATTN_LOOKAHEAD) % ATTN_SLOTS)
        softmax_pv(items[i], i % ATTN_SLOTS)


def _fox_attention(p_hm, qx, kx, *, batch, seq, n_heads, tq=512, tk=512, wide_tk=2048, hpb=2):
    t = batch * seq
    e_width = n_heads * F_HEAD_DIM
    assert n_heads % hpb == 0 and GATE_HEADS_PER_BLOCK % hpb == 0 and wide_tk % tk == 0
    body = functools.partial(_fox_attn_body, seq=seq, tq=tq, tk=tk, wide_tk=wide_tk, hpb=hpb)
    hm = lambda off: pl.BlockSpec((hpb, seq, F_HEAD_DIM), lambda b, p: (off // hpb + p, b, 0))
    xspec = pl.BlockSpec((1, 1, seq, LANES), lambda b, p: (b, (p * hpb) // GATE_HEADS_PER_BLOCK, 0, 0))
    return pl.pallas_call(
        body,
        grid=(batch, n_heads // hpb),
        in_specs=[hm(0), hm(n_heads), hm(2 * n_heads), hm(3 * n_heads), xspec, xspec],
        out_specs=pl.BlockSpec((seq, hpb * F_HEAD_DIM), lambda b, p: (b, p)),
        out_shape=jax.ShapeDtypeStruct((t, e_width), BF16),
        scratch_shapes=[
            pltpu.VMEM((hpb, seq, LANES), BF16),
            pltpu.VMEM((hpb, V_ROWS, seq), BF16),
            pltpu.VMEM((hpb, V_ROWS, tq), F32),
            pltpu.VMEM((ATTN_SLOTS, wide_tk, tq), F32),
        ],
        compiler_params=pltpu.CompilerParams(
            dimension_semantics=("parallel", "arbitrary"), vmem_limit_bytes=VMEM_LIMIT),
        name="fox_attention",
    )(p_hm, p_hm, p_hm, p_hm, qx, kx)


def kernel(x, norm_w, m_w_in, m_conv_w, m_conv_b, m_dt_bias, m_A_log, m_D, m_norm_w, m_w_out,
           f_w_in, f_b_forget, f_w_out, final_norm_w):
    batch, seq, d_model = x.shape
    t = batch * seq
    e_width = m_w_out.shape[1]
    m_heads = m_dt_bias.shape[1]
    f_heads = f_b_forget.shape[1]
    conv_dim = m_conv_w.shape[2]
    assert norm_w.shape[0] == 2 and m_w_in.shape[0] == 1 and f_w_in.shape[0] == 1
    assert e_width == m_heads * M_HEAD_DIM == f_heads * F_HEAD_DIM

    xf = x.reshape(t, d_model).astype(F32)

    w_in = m_w_in[0].T
    n_main = e_width + conv_dim
    w_dt = jnp.pad(w_in[n_main:, :], ((0, LANES - m_heads), (0, 0))).astype(BF16)
    zx, dt_raw, dtt_raw = _norm_matmul(xf, norm_w[0], w_in, n_main, w_dt, head_major=False,
                                       with_transposed=True)
    yg = _ssd(zx, dt_raw, dtt_raw, m_conv_w[0], m_conv_b[0], m_dt_bias[0], m_A_log[0], m_D[0],
              m_norm_w[0], batch=batch, seq=seq, e_width=e_width)
    x1 = _out_proj(yg, m_w_out[0].astype(BF16), xf, None)

    fw_in = f_w_in[0].T
    fw_f = jnp.pad(fw_in[4 * e_width:, :], ((0, LANES - f_heads), (0, 0))).astype(BF16)
    p_hm, f_raw = _norm_matmul(x1, norm_w[1], fw_in, 4 * e_width, fw_f, head_major=True,
                               with_transposed=False,
                               lead_cols=e_width, lead_scale=LOG2E / math.sqrt(F_HEAD_DIM))
    qx, kx = _forget_cumsum(f_raw, f_b_forget[0], batch=batch, seq=seq)
    og = _fox_attention(p_hm, qx, kx, batch=batch, seq=seq, n_heads=f_heads)
    out = _out_proj(og, f_w_out[0].astype(BF16), x1, final_norm_w)
    return out.reshape(batch, seq, d_model)
```

```python
import functools
import math

import jax
import jax.numpy as jnp
from jax import lax
from jax.experimental import pallas as pl
from jax.experimental.pallas import tpu as pltpu

F32 = jnp.float32
BF16 = jnp.bfloat16
EPS = 1e-6
LANES = 128
LOG2E = 1.4426950408889634
NEG_INF = float("-inf")
HIGHEST = lax.Precision.HIGHEST
VMEM_LIMIT = 56 * 1024 * 1024

M_HEAD_DIM = 64
M_GROUPS = 8
M_STATE = 128
M_CONV = 4
SSD_CHUNK = 128
CONV_TAIL = 16
SSD_LANE_CHUNK = 512
F_HEAD_DIM = 128

_NT = (((1,), (1,)), ((), ()))


def _sigmoid(v):
    return 1.0 / (1.0 + jnp.exp(-v))


def _softplus(v):
    return jnp.maximum(v, 0.0) + jnp.log(1.0 + jnp.exp(-jnp.abs(v)))


def _log_sigmoid(v):
    return jnp.minimum(v, 0.0) - jnp.log1p(jnp.exp(-jnp.abs(v)))


def _norm_matmul_body(x_ref, nw_ref, wt_ref, wst_ref, o_ref, os_ref, *rest,
                      tm, tn, head_major, row_chunk, with_transposed, lead_tiles, lead_scale):
    if with_transposed:
        ost_ref, h_scr = rest
    else:
        (h_scr,) = rest
    j = pl.program_id(1)

    @pl.when(j == 0)
    def _():
        nw = nw_ref[...]

        def norm_rows(r, carry):
            rows = pl.ds(pl.multiple_of(r * row_chunk, row_chunk), row_chunk)
            xv = x_ref[rows, :]
            ms = jnp.mean(xv * xv, axis=-1, keepdims=True)
            h_scr[rows, :] = ((xv * lax.rsqrt(ms + EPS)) * nw).astype(BF16)
            return carry

        lax.fori_loop(0, tm // row_chunk, norm_rows, 0, unroll=4)
        small = lax.dot_general(h_scr[...], wst_ref[...], _NT, preferred_element_type=F32)
        os_ref[...] = small
        if with_transposed:
            ost_ref[...] = small.T

    res = lax.dot_general(h_scr[...], wt_ref[...].astype(BF16), _NT, preferred_element_type=F32)
    if lead_tiles:
        res = res * jnp.where(j < lead_tiles, jnp.float32(lead_scale), jnp.float32(1.0))
    res = res.astype(o_ref.dtype)
    if head_major:
        for c in range(tn // LANES):
            o_ref[c] = res[:, c * LANES:(c + 1) * LANES]
    else:
        o_ref[...] = res


def _norm_matmul(x2d, nw, wt_all, n, wt_small, *, head_major, with_transposed, lead_cols=0,
                 lead_scale=1.0, tm=1024, tn=1024):
    t, d = x2d.shape
    assert t % tm == 0 and n % tn == 0 and wt_small.shape == (LANES, d) and lead_cols % tn == 0
    assert wt_all.shape[1] == d and wt_all.shape[0] >= n
    grid = (t // tm, n // tn)
    if head_major:
        main_shape = jax.ShapeDtypeStruct((n // LANES, t, LANES), BF16)
        main_spec = pl.BlockSpec((tn // LANES, tm, LANES), lambda i, j: (j, i, 0))
    else:
        main_shape = jax.ShapeDtypeStruct((t, n), BF16)
        main_spec = pl.BlockSpec((tm, tn), lambda i, j: (i, j))
    out_specs = [main_spec, pl.BlockSpec((tm, LANES), lambda i, j: (i, 0))]
    out_shape = [main_shape, jax.ShapeDtypeStruct((t, LANES), F32)]
    if with_transposed:
        out_specs.append(pl.BlockSpec((LANES, tm), lambda i, j: (0, i)))
        out_shape.append(jax.ShapeDtypeStruct((LANES, t), F32))
    body = functools.partial(_norm_matmul_body, tm=tm, tn=tn, head_major=head_major, row_chunk=64,
                             with_transposed=with_transposed, lead_tiles=lead_cols // tn,
                             lead_scale=lead_scale)
    return pl.pallas_call(
        body,
        grid=grid,
        in_specs=[
            pl.BlockSpec((tm, d), lambda i, j: (i, 0)),
            pl.BlockSpec((1, d), lambda i, j: (0, 0)),
            pl.BlockSpec((tn, d), lambda i, j: (j, 0)),
            pl.BlockSpec((LANES, d), lambda i, j: (0, 0)),
        ],
        out_specs=out_specs,
        out_shape=out_shape,
        scratch_shapes=[pltpu.VMEM((tm, d), BF16)],
        compiler_params=pltpu.CompilerParams(
            dimension_semantics=("parallel", "arbitrary"), vmem_limit_bytes=VMEM_LIMIT),
        name="norm_in_proj_hm" if head_major else "norm_in_proj",
    )(x2d, nw.reshape(1, d), wt_all, wt_small)


def _ssd_body(z_ref, x_ref, b_ref, c_ref, dt_ref, dtt_ref, cw_ref, cbias_ref,
              dtb_r_ref, dtb_c_ref, alog_r_ref, alog_c_ref, dskip_ref, nw_ref,
              o_ref,
              ext, xc, st, ysc, acs_s, acst_s, wt_s, eacs_s, cd_s,
              *, e_width, gn):
    q = SSD_CHUNK
    conv_dim = e_width + 2 * gn
    cidx = pl.program_id(1)

    @pl.when(cidx == 0)
    def _():
        ext[0:CONV_TAIL, :] = jnp.zeros((CONV_TAIL, conv_dim), BF16)
        st[...] = jnp.zeros(st.shape, F32)

    ext[pl.ds(CONV_TAIL, q), 0:e_width] = x_ref[...]
    ext[pl.ds(CONV_TAIL, q), e_width:e_width + gn] = b_ref[...]
    ext[pl.ds(CONV_TAIL, q), e_width + gn:conv_dim] = c_ref[...]

    n_sh = M_CONV - 1
    prow = lax.broadcasted_iota(jnp.int32, (n_sh * q, q + CONV_TAIL), 0)
    pcol = lax.broadcasted_iota(jnp.int32, (n_sh * q, q + CONV_TAIL), 1)
    assert q & (q - 1) == 0
    tap = prow >> (q.bit_length() - 1)
    shift_sel = jnp.where(pcol == (prow & (q - 1)) + tap + (CONV_TAIL - n_sh), 1.0, 0.0).astype(BF16)
    lane_chunk = SSD_LANE_CHUNK
    for i in range(conv_dim // lane_chunk):
        lanes = slice(i * lane_chunk, (i + 1) * lane_chunk)
        shifted = jnp.dot(shift_sel, ext[:, lanes], preferred_element_type=F32)
        acc = cbias_ref[:, lanes] + cw_ref[n_sh:M_CONV, lanes] * ext[pl.ds(CONV_TAIL, q), lanes].astype(F32)
        for k in range(n_sh):
            acc = acc + cw_ref[k:k + 1, lanes] * shifted[k * q:(k + 1) * q]
        xc[:, lanes] = acc * _sigmoid(acc)
    ext[0:CONV_TAIL, :] = ext[pl.ds(q, CONV_TAIL), :]

    li = lax.broadcasted_iota(jnp.int32, (q, q), 0)
    lj = lax.broadcasted_iota(jnp.int32, (q, q), 1)
    causal = lj <= li
    tril = causal.astype(F32)
    triu = (li <= lj).astype(F32)

    dt = _softplus(dt_ref[...] + dtb_r_ref[...])
    dtt = _softplus(dtt_ref[...] + dtb_c_ref[...])
    a_r = -jnp.exp(alog_r_ref[...])
    a_c = -jnp.exp(alog_c_ref[...])
    acs = jnp.dot(tril, dt * a_r, preferred_element_type=F32, precision=HIGHEST)
    acst = jnp.dot(dtt * a_c, triu, preferred_element_type=F32, precision=HIGHEST)
    acs_s[...] = acs * LOG2E
    acst_s[...] = (acst - jnp.log(dtt)) * LOG2E
    wt_s[...] = jnp.exp(acst[:, q - 1:q] - acst) * dtt
    eacs_s[...] = jnp.exp(acs)
    cd_s[...] = jnp.exp(acs[q - 1:q, :])

    lane = lax.broadcasted_iota(jnp.int32, (q, LANES), 1)
    lo_half = lane < M_HEAD_DIM
    lane_row = lax.broadcasted_iota(jnp.int32, (1, LANES), 1)
    lo_half_row = lane_row < M_HEAD_DIM
    hpg = e_width // M_HEAD_DIM // M_GROUPS
    gw = hpg * M_HEAD_DIM

    for g in range(M_GROUPS):
        bg = xc[:, e_width + g * M_STATE:e_width + (g + 1) * M_STATE]
        cg = xc[:, e_width + gn + g * M_STATE:e_width + gn + (g + 1) * M_STATE]
        bgb = bg.astype(BF16)
        cgb = cg.astype(BF16)
        cb = lax.dot_general(cgb, bgb, _NT, preferred_element_type=F32)
        bgt = bg.T
        for pr in range(hpg // 2):
            h0 = g * hpg + 2 * pr
            c0 = g * gw + pr * LANES
            xs_f = xc[:, c0:c0 + LANES]
            xs_b = xs_f.astype(BF16)
            zero = jnp.zeros_like(xs_b)
            rhs = jnp.concatenate([jnp.where(lo_half, xs_b, zero),
                                   jnp.where(lo_half, zero, xs_b)], axis=0)
            m_parts = []
            w_parts = []
            for hh in (h0, h0 + 1):
                seg = acs_s[:, hh:hh + 1] - acst_s[hh:hh + 1, :]
                lm = jnp.exp2(jnp.where(causal, seg, NEG_INF))
                m_parts.append((cb * lm).astype(BF16))
                w_parts.append((bgt * wt_s[hh:hh + 1, :]).astype(BF16))
            lhs = jnp.concatenate([jnp.concatenate(m_parts, axis=1),
                                   jnp.concatenate(w_parts, axis=1)], axis=0)
            res = jnp.dot(lhs, rhs, preferred_element_type=F32)
            y_diag = res[0:q]
            s_new = res[q:2 * q]
            e_pair = jnp.where(lo_half, eacs_s[:, h0:h0 + 1], eacs_s[:, h0 + 1:h0 + 2])
            cs = jnp.dot(cgb, st[:, c0:c0 + LANES].astype(BF16), preferred_element_type=F32)
            y_pair = (y_diag + e_pair * cs
                      + xs_f * dskip_ref[:, c0:c0 + LANES])
            cd_pair = jnp.where(lo_half_row, cd_s[:, h0:h0 + 1], cd_s[:, h0 + 1:h0 + 2])
            st[:, c0:c0 + LANES] = st[:, c0:c0 + LANES] * cd_pair + s_new
            ysc[:, c0:c0 + LANES] = y_pair
        zg = z_ref[:, g * gw:(g + 1) * gw].astype(F32)
        gt = ysc[:, g * gw:(g + 1) * gw] * (zg * _sigmoid(zg))
        ms = jnp.mean(gt * gt, axis=-1, keepdims=True)
        o_ref[:, g * gw:(g + 1) * gw] = (
            (gt * lax.rsqrt(ms + EPS)) * nw_ref[:, g * gw:(g + 1) * gw]).astype(BF16)


def _ssd(zx, dt_raw, dtt_raw, conv_w, conv_b, dt_bias, a_log, d_skip, norm_w, *, batch, seq, e_width):
    t = batch * seq
    gn = M_GROUPS * M_STATE
    conv_dim = e_width + 2 * gn
    q = SSD_CHUNK
    nc = seq // q
    n_heads = e_width // M_HEAD_DIM
    assert zx.shape == (t, e_width + conv_dim) and seq % q == 0
    assert e_width % gn == 0 and gn % LANES == 0

    def pad_row(v):
        return jnp.pad(v.astype(F32), (0, LANES - n_heads)).reshape(1, LANES)

    dtb_r = pad_row(dt_bias)
    alog_r = pad_row(a_log)
    dskip = jnp.repeat(d_skip.astype(F32), M_HEAD_DIM).reshape(1, e_width)
    rowblk = lambda b, c: b * nc + c
    const = lambda b, c: (0, 0)
    body = functools.partial(_ssd_body, e_width=e_width, gn=gn)
    return pl.pallas_call(
        body,
        grid=(batch, nc),
        in_specs=[
            pl.BlockSpec((q, e_width), lambda b, c: (rowblk(b, c), 0)),
            pl.BlockSpec((q, e_width), lambda b, c: (rowblk(b, c), 1)),
            pl.BlockSpec((q, gn), lambda b, c: (rowblk(b, c), 2 * e_width // gn)),
            pl.BlockSpec((q, gn), lambda b, c: (rowblk(b, c), 2 * e_width // gn + 1)),
            pl.BlockSpec((q, LANES), lambda b, c: (rowblk(b, c), 0)),
            pl.BlockSpec((LANES, q), lambda b, c: (0, rowblk(b, c))),
            pl.BlockSpec((M_CONV, conv_dim), const),
            pl.BlockSpec((1, conv_dim), const),
            pl.BlockSpec((1, LANES), const),
            pl.BlockSpec((LANES, 1), const),
            pl.BlockSpec((1, LANES), const),
            pl.BlockSpec((LANES, 1), const),
            pl.BlockSpec((1, e_width), const),
            pl.BlockSpec((1, e_width), const),
        ],
        out_specs=pl.BlockSpec((q, e_width), lambda b, c: (rowblk(b, c), 0)),
        out_shape=jax.ShapeDtypeStruct((t, e_width), BF16),
        scratch_shapes=[
            pltpu.VMEM((q + CONV_TAIL, conv_dim), BF16),
            pltpu.VMEM((q, conv_dim), F32),
            pltpu.VMEM((M_STATE, e_width), F32),
            pltpu.VMEM((q, e_width), F32),
            pltpu.VMEM((q, LANES), F32),
            pltpu.VMEM((LANES, q), F32),
            pltpu.VMEM((LANES, q), F32),
            pltpu.VMEM((q, LANES), F32),
            pltpu.VMEM((1, LANES), F32),
        ],
        compiler_params=pltpu.CompilerParams(
            dimension_semantics=("parallel", "arbitrary"), vmem_limit_bytes=VMEM_LIMIT),
        name="ssd_mixer",
    )(zx, zx, zx, zx, dt_raw, dtt_raw, conv_w.astype(F32), conv_b.astype(F32).reshape(1, conv_dim),
      dtb_r, dtb_r.reshape(LANES, 1), alog_r, alog_r.reshape(LANES, 1), dskip,
      norm_w.astype(F32).reshape(1, e_width))


def _out_proj_body(g_ref, w_ref, x_ref, fw_ref, o_ref, *, tn, final_norm):
    d = o_ref.shape[1]
    for c in range(d // tn):
        cols = slice(c * tn, (c + 1) * tn)
        o_ref[:, cols] = x_ref[:, cols] + jnp.dot(g_ref[...], w_ref[:, cols],
                                                  preferred_element_type=F32)
    if final_norm:
        r = o_ref[...]
        ms = jnp.mean(r * r, axis=-1, keepdims=True)
        o_ref[...] = (r * lax.rsqrt(ms + EPS)) * fw_ref[...]


def _out_proj(g, w, x2d, final_w, *, tm=512, tn=512):
    t, kdim = g.shape
    d = w.shape[1]
    assert t % tm == 0 and d % tn == 0
    final_norm = final_w is not None
    fw = (final_w if final_norm else jnp.ones((d,), F32)).astype(F32).reshape(1, d)
    body = functools.partial(_out_proj_body, tn=tn, final_norm=final_norm)
    return pl.pallas_call(
        body,
        grid=(t // tm,),
        in_specs=[
            pl.BlockSpec((tm, kdim), lambda i: (i, 0)),
            pl.BlockSpec((kdim, d), lambda i: (0, 0), pipeline_mode=pl.Buffered(1)),
            pl.BlockSpec((tm, d), lambda i: (i, 0)),
            pl.BlockSpec((1, d), lambda i: (0, 0)),
        ],
        out_specs=pl.BlockSpec((tm, d), lambda i: (i, 0)),
        out_shape=jax.ShapeDtypeStruct((t, d), F32),
        compiler_params=pltpu.CompilerParams(
            dimension_semantics=("parallel",), vmem_limit_bytes=VMEM_LIMIT),
        name="out_proj_final" if final_norm else "out_proj",
    )(g, w, x2d, fw)


GATE_COLS = 8
GATE_HEADS_PER_BLOCK = LANES // GATE_COLS


def _forget_cumsum_body(f_ref, b_r_ref, qx_ref, kx_ref, carry_r, *, blk, n_heads):
    i = pl.program_id(1)

    @pl.when(i == 0)
    def _():
        carry_r[...] = jnp.zeros(carry_r.shape, F32)

    li = lax.broadcasted_iota(jnp.int32, (blk, blk), 0)
    lj = lax.broadcasted_iota(jnp.int32, (blk, blk), 1)
    tril = (lj <= li).astype(F32)
    lf = _log_sigmoid(f_ref[...] + b_r_ref[...])
    cum = jnp.dot(tril, lf, preferred_element_type=F32, precision=HIGHEST) + carry_r[...]
    carry_r[...] = cum[blk - 1:blk, :]
    c2 = cum * LOG2E
    hi = c2.astype(BF16)
    r1 = c2 - hi.astype(F32)
    mid = r1.astype(BF16)
    lo = (r1 - mid.astype(F32)).astype(BF16)
    parts = jnp.concatenate([hi, mid, lo], axis=1)
    wide = 2 * LANES
    row = lax.broadcasted_iota(jnp.int32, (3 * LANES, wide), 0)
    col = lax.broadcasted_iota(jnp.int32, (3 * LANES, wide), 1)
    head = row & (LANES - 1)
    part = row >> 7
    live = head < n_heads
    e_q = jnp.where((col == head * GATE_COLS + part) & live, 1.0, 0.0).astype(BF16)
    e_k = jnp.where((col == head * GATE_COLS + 3 + part) & live, 1.0, 0.0).astype(BF16)
    lane = lax.broadcasted_iota(jnp.int32, (1, wide), 1)
    pos = lane & (GATE_COLS - 1)
    valid = (lane >> 3) < n_heads
    ones_q = jnp.where((pos >= 3) & (pos < 6) & valid, 1.0, 0.0)
    ones_k = jnp.where((pos < 3) & valid, 1.0, 0.0)
    qx = jnp.dot(parts, e_q, preferred_element_type=F32) + ones_q
    kx = ones_k - jnp.dot(parts, e_k, preferred_element_type=F32)
    for half in range(2):
        qx_ref[0, half] = qx[:, half * LANES:(half + 1) * LANES].astype(BF16)
        kx_ref[0, half] = kx[:, half * LANES:(half + 1) * LANES].astype(BF16)


def _forget_cumsum(f, b_forget, *, batch, seq, blk=512):
    nb = seq // blk
    n_heads = b_forget.shape[0]
    assert n_heads * GATE_COLS <= 2 * LANES
    b_r = jnp.pad(b_forget.astype(F32), (0, LANES - n_heads)).reshape(1, LANES)
    body = functools.partial(_forget_cumsum_body, blk=blk, n_heads=n_heads)
    xspec = pl.BlockSpec((1, 2, blk, LANES), lambda b, i: (b, 0, i, 0))
    xshape = jax.ShapeDtypeStruct((batch, 2, seq, LANES), BF16)
    return pl.pallas_call(
        body,
        grid=(batch, nb),
        in_specs=[
            pl.BlockSpec((blk, LANES), lambda b, i: (b * nb + i, 0)),
            pl.BlockSpec((1, LANES), lambda b, i: (0, 0)),
        ],
        out_specs=[xspec, xspec],
        out_shape=[xshape, xshape],
        scratch_shapes=[pltpu.VMEM((1, LANES), F32)],
        compiler_params=pltpu.CompilerParams(
            dimension_semantics=("parallel", "arbitrary"), vmem_limit_bytes=VMEM_LIMIT),
        name="forget_cumsum",
    )(f, b_r)


V_ROWS = F_HEAD_DIM + 16
ATTN_LOOKAHEAD = 2
ATTN_SLOTS = ATTN_LOOKAHEAD + 1


def _fox_attn_body(q_ref, k_ref, v_ref, z_ref, qx_ref, kx_ref, o_ref,
                   kxm_scr, vt_scr, acc_scr, st_scr, *, seq, tq, tk, wide_tk, hpb):
    pair = pl.program_id(1)
    nq = seq // tq
    half = tq // 2
    assert tq == tk and seq % tq == 0

    lane = lax.broadcasted_iota(jnp.int32, (seq, LANES), 1)
    kxf = kx_ref[0, 0].astype(F32)
    r16 = lax.broadcasted_iota(jnp.int32, (V_ROWS - F_HEAD_DIM, seq), 0)
    ones_rows = jnp.where(r16 == 0, 1.0, 0.0).astype(BF16)
    for hh in range(hpb):
        mine = (lane >> 3) == ((pair * hpb + hh) % GATE_HEADS_PER_BLOCK)
        kxm_scr[hh] = jnp.where(mine, kxf, 0.0).astype(BF16)
        vt_scr[hh, F_HEAD_DIM:V_ROWS, :] = ones_rows

    kpos = lax.broadcasted_iota(jnp.int32, (tk, tq), 0)
    qpos = lax.broadcasted_iota(jnp.int32, (tk, tq), 1)
    causal = kpos <= qpos

    items = []
    for qi in range(nq):
        units = [("diag", qi * tk, half, 0, half), ("diag", qi * tk, tq, half, tq)]
        pos = 0
        while pos < qi * tk:
            kw = min(wide_tk, qi * tk - pos)
            units.append(("below", pos, kw, 0, tq))
            pos += kw
        for ui, unit in enumerate(units):
            for hh in range(hpb):
                items.append((qi, hh, ui, len(units)) + unit)

    qat = {}
    m_run = {}
    m_diag = {}

    def setup(qi):
        qrows = pl.ds(qi * tq, tq)
        for hh in range(hpb):
            for c in range(tq // LANES):
                rows = pl.ds(qi * tq + c * LANES, LANES)
                vt_scr[hh, 0:F_HEAD_DIM, rows] = v_ref[hh, rows, :].astype(F32).T.astype(BF16)
            qcat = jnp.concatenate([q_ref[hh, qrows, :], qx_ref[0, 0, qrows, :]], axis=1)
            qat[(qi, hh)] = qcat.astype(F32).T.astype(BF16)

    def logits(item, slot):
        qi, hh, ui, nu, kind, k0, kw, c0, c1 = item
        if ui == 0 and hh == 0:
            setup(qi)
        krows = pl.ds(k0, kw)
        ka = jnp.concatenate([k_ref[hh, krows, :], kxm_scr[hh, krows, :]], axis=1)
        st = jnp.dot(ka, qat[(qi, hh)][:, c0:c1], preferred_element_type=F32)
        if kind == "diag":
            st = jnp.where(causal[0:kw, c0:c1], st, NEG_INF)
        st_scr[slot, 0:kw, 0:c1 - c0] = st

    def softmax_pv(item, slot):
        qi, hh, ui, nu, kind, k0, kw, c0, c1 = item
        krows = pl.ds(k0, kw)
        st = st_scr[slot, 0:kw, 0:c1 - c0]
        mb = jnp.max(st, axis=0, keepdims=True)
        if kind == "diag":
            pt = jnp.exp2(st - mb).astype(BF16)
            acc_scr[hh, :, c0:c1] = jnp.dot(vt_scr[hh, :, krows], pt, preferred_element_type=F32)
            m_diag.setdefault((qi, hh), []).append(mb)
            if c1 == tq:
                m_run[(qi, hh)] = jnp.concatenate(m_diag[(qi, hh)], axis=1)
        else:
            m_old = m_run[(qi, hh)]
            m_new = jnp.maximum(m_old, mb)
            pt = jnp.exp2(st - m_new).astype(BF16)
            pv = jnp.dot(vt_scr[hh, :, krows], pt, preferred_element_type=F32)
            acc_scr[hh] = acc_scr[hh] * jnp.exp2(m_old - m_new) + pv
            m_run[(qi, hh)] = m_new
        if ui == nu - 1:
            qrows = pl.ds(qi * tq, tq)
            ot = acc_scr[hh, 0:F_HEAD_DIM, :] / acc_scr[hh, F_HEAD_DIM:F_HEAD_DIM + 1, :]
            zv = z_ref[hh, qrows, :].astype(F32)
            o_ref[qrows, hh * F_HEAD_DIM:(hh + 1) * F_HEAD_DIM] = (
                ot.T * (zv * _sigmoid(zv))).astype(BF16)

    n_items = len(items)
    for i in range(min(ATTN_LOOKAHEAD, n_items)):
        logits(items[i], i % ATTN_SLOTS)
    for i in range(n_items):
        if i + ATTN_LOOKAHEAD < n_items:
            logits(items[i + ATTN_LOOKAHEAD], (i + ATTN_LOOKAHEAD) % ATTN_SLOTS)
        softmax_pv(items[i], i % ATTN_SLOTS)


def _fox_attention(p_hm, qx, kx, *, batch, seq, n_heads, tq=512, tk=512, wide_tk=2048, hpb=2):
    t = batch * seq
    e_width = n_heads * F_HEAD_DIM
    assert n_heads % hpb == 0 and GATE_HEADS_PER_BLOCK % hpb == 0 and wide_tk % tk == 0
    body = functools.partial(_fox_attn_body, seq=seq, tq=tq, tk=tk, wide_tk=wide_tk, hpb=hpb)
    hm = lambda off: pl.BlockSpec((hpb, seq, F_HEAD_DIM), lambda b, p: (off // hpb + p, b, 0))
    xspec = pl.BlockSpec((1, 1, seq, LANES), lambda b, p: (b, (p * hpb) // GATE_HEADS_PER_BLOCK, 0, 0))
    return pl.pallas_call(
        body,
        grid=(batch, n_heads // hpb),
        in_specs=[hm(0), hm(n_heads), hm(2 * n_heads), hm(3 * n_heads), xspec, xspec],
        out_specs=pl.BlockSpec((seq, hpb * F_HEAD_DIM), lambda b, p: (b, p)),
        out_shape=jax.ShapeDtypeStruct((t, e_width), BF16),
        scratch_shapes=[
            pltpu.VMEM((hpb, seq, LANES), BF16),
            pltpu.VMEM((hpb, V_ROWS, seq), BF16),
            pltpu.VMEM((hpb, V_ROWS, tq), F32),
            pltpu.VMEM((ATTN_SLOTS, wide_tk, tq), F32),
        ],
        compiler_params=pltpu.CompilerParams(
            dimension_semantics=("parallel", "arbitrary"), vmem_limit_bytes=VMEM_LIMIT),
        name="fox_attention",
    )(p_hm, p_hm, p_hm, p_hm, qx, kx)


def kernel(x, norm_w, m_w_in, m_conv_w, m_conv_b, m_dt_bias, m_A_log, m_D, m_norm_w, m_w_out,
           f_w_in, f_b_forget, f_w_out, final_norm_w):
    batch, seq, d_model = x.shape
    t = batch * seq
    e_width = m_w_out.shape[1]
    m_heads = m_dt_bias.shape[1]
    f_heads = f_b_forget.shape[1]
    conv_dim = m_conv_w.shape[2]
    assert norm_w.shape[0] == 2 and m_w_in.shape[0] == 1 and f_w_in.shape[0] == 1
    assert e_width == m_heads * M_HEAD_DIM == f_heads * F_HEAD_DIM

    xf = x.reshape(t, d_model).astype(F32)

    w_in = m_w_in[0].T
    n_main = e_width + conv_dim
    w_dt = jnp.pad(w_in[n_main:, :], ((0, LANES - m_heads), (0, 0))).astype(BF16)
    zx, dt_raw, dtt_raw = _norm_matmul(xf, norm_w[0], w_in, n_main, w_dt, head_major=False,
                                       with_transposed=True)
    yg = _ssd(zx, dt_raw, dtt_raw, m_conv_w[0], m_conv_b[0], m_dt_bias[0], m_A_log[0], m_D[0],
              m_norm_w[0], batch=batch, seq=seq, e_width=e_width)
    x1 = _out_proj(yg, m_w_out[0].astype(BF16), xf, None)

    fw_in = f_w_in[0].T
    fw_f = jnp.pad(fw_in[4 * e_width:, :], ((0, LANES - f_heads), (0, 0))).astype(BF16)
    p_hm, f_raw = _norm_matmul(x1, norm_w[1], fw_in, 4 * e_width, fw_f, head_major=True,
                               with_transposed=False,
                               lead_cols=e_width, lead_scale=LOG2E / math.sqrt(F_HEAD_DIM))
    qx, kx = _forget_cumsum(f_raw, f_b_forget[0], batch=batch, seq=seq)
    og = _fox_attention(p_hm, qx, kx, batch=batch, seq=seq, n_heads=f_heads)
    out = _out_proj(og, f_w_out[0].astype(BF16), x1, final_norm_w)
    return out.reshape(batch, seq, d_model)
```
